```python
import math
import jax, jax.numpy as jnp
from jax import lax
import numpy as np

D_MODEL = 1024
BATCH = 8
SEQ = 2048
DEPTH = 4

N_MEM = 256
HEAD_DIM = 64
N_HEADS = D_MODEL // HEAD_DIM
ROPE_THETA = 500000.0
ROPE_DIM = HEAD_DIM // 4
Q_BLOCK = 128
N_MIXERS = 4

NSA_KV_GROUPS = 4
NSA_CMP_BLOCK = 32
NSA_CMP_STRIDE = 16
NSA_CMP_HIDDEN = 2 * HEAD_DIM
NSA_SLC_BLOCK = 64
NSA_SLC_TOPN = 8
NSA_WINDOW = 512
NSA_QCHUNK = 64
NSA_FORCE_SCORE = 1e4

MLA_Q_RANK = D_MODEL // 4
MLA_KV_RANK = D_MODEL // 8
MLA_NOPE_DIM = 64
MLA_ROPE_DIM = 32
MLA_V_DIM = 64

MOBA_BLOCK = 256
MOBA_TOPK = 3
MOBA_QCHUNK = 8

SWA_KV_HEADS = 2
SWA_WINDOW = 128

XATTN_HEADS = 4
XATTN_HEAD_DIM = D_MODEL // XATTN_HEADS

D_FF = (D_MODEL * 7) // 2
N_EXPERTS = 8
TOP_K = 2
D_FF_EXPERT = D_FF

DEEPNORM_ALPHA = (2.0 * DEPTH) ** 0.25
DEEPNORM_BETA = (8.0 * DEPTH) ** -0.25
LN_EPS = 1e-5
RMS_EPS = 1e-6

kernel_name = "hybrid_nsa_mla_moba_swa_deepnorm_moe"


def layer_norm(x, g, b):
    xf = x.astype(jnp.float32)
    mu = jnp.mean(xf, axis=-1, keepdims=True)
    var = jnp.mean(jnp.square(xf - mu), axis=-1, keepdims=True)
    y = (xf - mu) * lax.rsqrt(var + LN_EPS)
    return (y * g.astype(jnp.float32) + b.astype(jnp.float32)).astype(x.dtype)


def rms_norm(x, g):
    xf = x.astype(jnp.float32)
    y = xf * lax.rsqrt(jnp.mean(jnp.square(xf), axis=-1, keepdims=True) + RMS_EPS)
    return (y * g.astype(jnp.float32)).astype(x.dtype)


def rope_tables(positions, dim):
    inv = ROPE_THETA ** (-jnp.arange(0, dim, 2, dtype=jnp.float32) / dim)
    ang = positions.astype(jnp.float32)[..., None] * inv
    return jnp.cos(ang), jnp.sin(ang)


def apply_rope(x, cos, sin):
    rd = 2 * cos.shape[-1]
    x1, x2, xp = x[..., : rd // 2], x[..., rd // 2: rd], x[..., rd:]
    c = cos[:, None].astype(x.dtype)
    s = sin[:, None].astype(x.dtype)
    return jnp.concatenate([x1 * c - x2 * s, x2 * c + x1 * s, xp], axis=-1)


def masked_softmax(s, mask, sink=None):
    s = jnp.where(mask, s.astype(jnp.float32), -jnp.inf)
    m = jnp.max(s, axis=-1, keepdims=True)
    if sink is not None:
        m = jnp.maximum(m, sink)
    m = jnp.where(jnp.isfinite(m), m, 0.0)
    e = jnp.exp(s - m)
    denom = jnp.sum(e, axis=-1, keepdims=True)
    if sink is not None:
        denom = denom + jnp.exp(sink - m)
    return e / jnp.maximum(denom, 1e-30)


def causal_attention(q, k, v, scale):
    B, G, R, S, dk = q.shape
    nq = S // Q_BLOCK
    qb = jnp.moveaxis(q.reshape(B, G, R, nq, Q_BLOCK, dk), 3, 0)
    kpos = jnp.arange(S)

    def block(args):
        i, qi = args
        qpos = i * Q_BLOCK + jnp.arange(Q_BLOCK)
        s = jnp.einsum('bgrqd,bgkd->bgrqk', qi, k) * scale
        p = masked_softmax(s, kpos[None, :] <= qpos[:, None])
        return jnp.einsum('bgrqk,bgkd->bgrqd', p.astype(v.dtype), v)

    out = lax.map(block, (jnp.arange(nq), qb))
    return jnp.moveaxis(out, 0, 3).reshape(B, G, R, S, v.shape[-1])


def banded_attention(q, k, v, window, scale, sink=None):
    B, G, R, S, dk = q.shape
    nq = S // Q_BLOCK
    pad = -(-window // Q_BLOCK) * Q_BLOCK
    span = pad + Q_BLOCK
    kp = jnp.pad(k, ((0, 0), (0, 0), (pad, 0), (0, 0)))
    vp = jnp.pad(v, ((0, 0), (0, 0), (pad, 0), (0, 0)))
    qb = jnp.moveaxis(q.reshape(B, G, R, nq, Q_BLOCK, dk), 3, 0)

    def block(args):
        i, qi = args
        start = i * Q_BLOCK
        ki = lax.dynamic_slice_in_dim(kp, start, span, axis=2)
        vi = lax.dynamic_slice_in_dim(vp, start, span, axis=2)
        qpos = start + jnp.arange(Q_BLOCK)
        kpos = start - pad + jnp.arange(span)
        delta = qpos[:, None] - kpos[None, :]
        mask = (delta >= 0) & (delta < window) & (kpos[None, :] >= 0)
        s = jnp.einsum('bgrqd,bgkd->bgrqk', qi, ki) * scale
        p = masked_softmax(s, mask, sink)
        return jnp.einsum('bgrqk,bgkd->bgrqd', p.astype(v.dtype), vi)

    out = lax.map(block, (jnp.arange(nq), qb))
    return jnp.moveaxis(out, 0, 3).reshape(B, G, R, S, v.shape[-1])


def nsa_mixer(x, cos, sin, w_in, cmp_pe, cmp_w1, cmp_w2, w_out):
    B, S, _ = x.shape
    H, G, dh = N_HEADS, NSA_KV_GROUPS, HEAD_DIM
    R = H // G
    scale = dh ** -0.5
    q, kv, gates = jnp.split(x @ w_in, [H * dh, H * dh + 6 * G * dh], axis=-1)
    q = apply_rope(q.reshape(B, S, H, dh).transpose(0, 2, 1, 3), cos, sin).reshape(B, G, R, S, dh)
    kv = kv.reshape(B, S, 6, G, dh).transpose(2, 0, 3, 1, 4)
    k_c, v_c = apply_rope(kv[0], cos, sin), kv[1]
    k_s, v_s = apply_rope(kv[2], cos, sin), kv[3]
    k_w, v_w = apply_rope(kv[4], cos, sin), kv[5]
    tpos = jnp.arange(S)

    nc = (S - NSA_CMP_BLOCK) // NSA_CMP_STRIDE + 1
    c_start = np.arange(nc) * NSA_CMP_STRIDE
    idx = c_start[:, None] + np.arange(NSA_CMP_BLOCK)[None, :]

    def compress(t, j):
        blocks = t[:, :, idx] + cmp_pe[j]
        h = jax.nn.gelu(blocks.reshape(B, G, nc, NSA_CMP_BLOCK * dh) @ cmp_w1[j])
        return h @ cmp_w2[j]

    kc, vc = compress(k_c, 0), compress(v_c, 1)
    cmp_end = jnp.asarray(c_start + NSA_CMP_BLOCK - 1)
    s_cmp = jnp.einsum('bgrsd,bgcd->bgrsc', q, kc) * scale
    p_cmp = masked_softmax(s_cmp, cmp_end[None, :] <= tpos[:, None])
    o_cmp = jnp.einsum('bgrsc,bgcd->bgrsd', p_cmp.astype(vc.dtype), vc)

    nb = S // NSA_SLC_BLOCK
    b_start = np.arange(nb) * NSA_SLC_BLOCK
    overlap = np.clip(np.minimum(c_start[:, None] + NSA_CMP_BLOCK, b_start[None, :] + NSA_SLC_BLOCK)
                      - np.maximum(c_start[:, None], b_start[None, :]), 0, None) / NSA_CMP_BLOCK
    imp = jnp.einsum('bgrsc,cj->bgsj', p_cmp, jnp.asarray(overlap, jnp.float32))
    blk = jnp.arange(nb)[None, :]
    cur = (tpos // NSA_SLC_BLOCK)[:, None]
    forced = (blk == 0) | (blk == cur) | (blk == cur - 1)
    score = jnp.where(blk <= cur, jnp.where(forced, NSA_FORCE_SCORE, imp), -jnp.inf)
    n_sel = min(NSA_SLC_TOPN, nb)
    _, sel = lax.top_k(score, n_sel)

    ks_blk = k_s.reshape(B, G, nb, NSA_SLC_BLOCK, dh)
    vs_blk = v_s.reshape(B, G, nb, NSA_SLC_BLOCK, dh)
    nqc = S // NSA_QCHUNK
    q_ch = jnp.moveaxis(q.reshape(B, G, R, nqc, NSA_QCHUNK, dh), 3, 0)
    sel_ch = jnp.moveaxis(sel.reshape(B, G, nqc, NSA_QCHUNK, n_sel), 2, 0)
    bi = jnp.arange(B)[:, None, None, None]
    gi = jnp.arange(G)[None, :, None, None]
    L = n_sel * NSA_SLC_BLOCK

    def chunk(args):
        c, qc, sc = args
        kg = ks_blk[bi, gi, sc].reshape(B, G, NSA_QCHUNK, L, dh)
        vg = vs_blk[bi, gi, sc].reshape(B, G, NSA_QCHUNK, L, dh)
        qpos = c * NSA_QCHUNK + jnp.arange(NSA_QCHUNK)
        kpos = (sc[..., None] * NSA_SLC_BLOCK + jnp.arange(NSA_SLC_BLOCK)).reshape(B, G, NSA_QCHUNK, L)
        mask = (kpos <= qpos[:, None])[:, :, None]
        s = jnp.einsum('bgrqd,bgqkd->bgrqk', qc, kg) * scale
        p = masked_softmax(s, mask)
        return jnp.einsum('bgrqk,bgqkd->bgrqd', p.astype(vg.dtype), vg)

    o_slc = lax.map(chunk, (jnp.arange(nqc), q_ch, sel_ch))
    o_slc = jnp.moveaxis(o_slc, 0, 3).reshape(B, G, R, S, dh)

    o_win = banded_attention(q, k_w, v_w, NSA_WINDOW, scale)

    g = jax.nn.sigmoid(gates).reshape(B, S, 3, G, R).transpose(2, 0, 3, 4, 1)[..., None]
    o = g[0] * o_cmp + g[1] * o_slc + g[2] * o_win
    return o.reshape(B, H, S, dh).transpose(0, 2, 1, 3).reshape(B, S, H * dh) @ w_out


def mla_mixer(x, cos_m, sin_m, w_down, q_norm, kv_norm, w_uq, w_ukv, w_out):
    B, S, _ = x.shape
    H = N_HEADS
    cq, ckv, k_rope = jnp.split(x @ w_down, [MLA_Q_RANK, MLA_Q_RANK + MLA_KV_RANK], axis=-1)
    q = (rms_norm(cq, q_norm) @ w_uq).reshape(B, S, H, MLA_NOPE_DIM + MLA_ROPE_DIM).transpose(0, 2, 1, 3)
    kv = (rms_norm(ckv, kv_norm) @ w_ukv).reshape(B, S, H, MLA_NOPE_DIM + MLA_V_DIM).transpose(0, 2, 1, 3)
    q = jnp.concatenate([q[..., :MLA_NOPE_DIM], apply_rope(q[..., MLA_NOPE_DIM:], cos_m, sin_m)], axis=-1)
    k_rope = apply_rope(k_rope[:, None], cos_m, sin_m)
    k = jnp.concatenate([kv[..., :MLA_NOPE_DIM],
                         jnp.broadcast_to(k_rope, (B, H, S, MLA_ROPE_DIM))], axis=-1)
    v = kv[..., MLA_NOPE_DIM:]
    scale = (MLA_NOPE_DIM + MLA_ROPE_DIM) ** -0.5
    o = causal_attention(q[:, :, None], k, v, scale)[:, :, 0]
    return o.transpose(0, 2, 1, 3).reshape(B, S, H * MLA_V_DIM) @ w_out


def moba_mixer(x, cos, sin, w_in, w_out):
    B, S, _ = x.shape
    H, dh = N_HEADS, HEAD_DIM
    scale = dh ** -0.5
    q, k, v = [t.reshape(B, S, H, dh).transpose(0, 2, 1, 3) for t in jnp.split(x @ w_in, 3, axis=-1)]
    q, k = apply_rope(q, cos, sin), apply_rope(k, cos, sin)
    nb = -(-S // MOBA_BLOCK)
    sp = nb * MOBA_BLOCK
    padw = ((0, 0), (0, 0), (0, sp - S), (0, 0))
    q, k, v = jnp.pad(q, padw), jnp.pad(k, padw), jnp.pad(v, padw)
    k_blk = k.reshape(B, H, nb, MOBA_BLOCK, dh)
    v_blk = v.reshape(B, H, nb, MOBA_BLOCK, dh)
    k_mean = jnp.mean(k_blk, axis=3)
    cur = (jnp.arange(sp) // MOBA_BLOCK)[:, None]
    gate = jnp.einsum('bhsd,bhjd->bhsj', q, k_mean)
    gate = jnp.where(jnp.arange(nb)[None, :] < cur, gate, -jnp.inf)
    n_sel = max(1, min(MOBA_TOPK, nb - 1))
    _, sel = lax.top_k(gate, n_sel)
    nqc = sp // MOBA_QCHUNK
    q_ch = jnp.moveaxis(q.reshape(B, H, nqc, MOBA_QCHUNK, dh), 2, 0)
    sel_ch = jnp.moveaxis(sel.reshape(B, H, nqc, MOBA_QCHUNK, n_sel), 2, 0)
    bi = jnp.arange(B)[:, None, None, None]
    hi = jnp.arange(H)[None, :, None, None]

    def chunk(args):
        c, qc, sc = args
        q0 = c * MOBA_QCHUNK
        own = q0 // MOBA_BLOCK
        qpos = q0 + jnp.arange(MOBA_QCHUNK)
        k_own = lax.dynamic_slice_in_dim(k, own * MOBA_BLOCK, MOBA_BLOCK, axis=2)
        v_own = lax.dynamic_slice_in_dim(v, own * MOBA_BLOCK, MOBA_BLOCK, axis=2)
        own_pos = own * MOBA_BLOCK + jnp.arange(MOBA_BLOCK)
        k_sel = k_blk[bi, hi, sc].reshape(B, H, MOBA_QCHUNK, n_sel * MOBA_BLOCK, dh)
        v_sel = v_blk[bi, hi, sc].reshape(B, H, MOBA_QCHUNK, n_sel * MOBA_BLOCK, dh)
        s = jnp.concatenate([jnp.einsum('bhqd,bhkd->bhqk', qc, k_own),
                             jnp.einsum('bhqd,bhqkd->bhqk', qc, k_sel)], axis=-1) * scale
        m_own = jnp.broadcast_to(own_pos[None, :] <= qpos[:, None], (B, H, MOBA_QCHUNK, MOBA_BLOCK))
        m_sel = jnp.repeat(sc < own, MOBA_BLOCK, axis=-1)
        p = masked_softmax(s, jnp.concatenate([m_own, m_sel], axis=-1)).astype(v.dtype)
        return (jnp.einsum('bhqk,bhkd->bhqd', p[..., :MOBA_BLOCK], v_own)
                + jnp.einsum('bhqk,bhqkd->bhqd', p[..., MOBA_BLOCK:], v_sel))

    o = lax.map(chunk, (jnp.arange(nqc), q_ch, sel_ch))
    o = jnp.moveaxis(o, 0, 2).reshape(B, H, sp, dh)[:, :, :S]
    return o.transpose(0, 2, 1, 3).reshape(B, S, H * dh) @ w_out


def swa_mixer(x, cos, sin, w_in, sinks, w_out):
    B, S, _ = x.shape
    H, G, dh = N_HEADS, SWA_KV_HEADS, HEAD_DIM
    R = H // G
    q, k, v = jnp.split(x @ w_in, [H * dh, H * dh + G * dh], axis=-1)
    q = apply_rope(q.reshape(B, S, H, dh).transpose(0, 2, 1, 3), cos, sin).reshape(B, G, R, S, dh)
    k = apply_rope(k.reshape(B, S, G, dh).transpose(0, 2, 1, 3), cos, sin)
    v = v.reshape(B, S, G, dh).transpose(0, 2, 1, 3)
    sink = sinks.astype(jnp.float32).reshape(1, G, R, 1, 1)
    o = banded_attention(q, k, v, SWA_WINDOW, dh ** -0.5, sink)
    return o.reshape(B, H, S, dh).transpose(0, 2, 1, 3).reshape(B, S, H * dh) @ w_out


def cross_attention(x, mem, w_q, w_kv, w_o):
    B, S, D = x.shape
    M = mem.shape[1]
    q = (x @ w_q).reshape(B, S, XATTN_HEADS, XATTN_HEAD_DIM)
    k, v = [t.reshape(B, M, XATTN_HEADS, XATTN_HEAD_DIM) for t in jnp.split(mem @ w_kv, 2, axis=-1)]
    s = jnp.einsum('bshd,bmhd->bhsm', q, k) * XATTN_HEAD_DIM ** -0.5
    p = jax.nn.softmax(s.astype(jnp.float32), axis=-1).astype(x.dtype)
    return jnp.einsum('bhsm,bmhd->bshd', p, v).reshape(B, S, D) @ w_o


def swiglu(x, w_in, w_out):
    g, u = jnp.split(x @ w_in, 2, axis=-1)
    return (jax.nn.silu(g) * u) @ w_out


def moe_ffn(x, router, router_bias, w_in, w_out):
    B, S, D = x.shape
    t = x.reshape(B * S, D)
    logits = (t @ router).astype(jnp.float32) + router_bias.astype(jnp.float32)
    top_val, top_idx = lax.top_k(logits, TOP_K)
    gates = jax.nn.softmax(top_val, axis=-1)
    comb = jnp.sum(jax.nn.one_hot(top_idx, N_EXPERTS, dtype=jnp.float32) * gates[..., None], axis=1)
    y = jnp.zeros_like(t)
    for e in range(N_EXPERTS):
        y = y + comb[:, e:e + 1].astype(t.dtype) * swiglu(t, w_in[e], w_out[e])
    return y.reshape(B, S, D)


def setup_inputs(seed: int = 0) -> dict:
    key = jax.random.key(seed)
    ks = iter(jax.random.split(key, 96))
    f32 = jnp.float32

    def nrm(shape, std):
        return jax.random.normal(next(ks), shape, f32) * std

    def gain(n):
        return 1.0 + nrm((n,), 0.05)

    D, H, dh, beta = D_MODEL, N_HEADS, HEAD_DIM, DEEPNORM_BETA
    inp = {}
    inp["x"] = nrm((BATCH, SEQ, D), 1.0)
    inp["mem"] = nrm((BATCH, N_MEM, D), 1.0)
    offset = jax.random.randint(next(ks), (BATCH, 1), 0, 4096, dtype=jnp.int32)
    inp["positions"] = offset + jnp.arange(SEQ, dtype=jnp.int32)[None, :]

    def common_attn(i):
        inp[f"l{i}_ln1_g"] = gain(D)
        inp[f"l{i}_ln1_b"] = nrm((D,), 0.02)
        inp[f"l{i}_xq"] = nrm((D, D), D ** -0.5)
        inp[f"l{i}_xkv"] = nrm((D, 2 * D), D ** -0.5)
        inp[f"l{i}_xo"] = nrm((D, D), D ** -0.5 * beta)
        inp[f"l{i}_ln2_g"] = gain(D)
        inp[f"l{i}_ln2_b"] = nrm((D,), 0.02)

    def dense_ffn(i):
        inp[f"l{i}_ffn_w_in"] = nrm((D, 2 * D_FF), D ** -0.5)
        inp[f"l{i}_ffn_w_out"] = nrm((D_FF, D), D_FF ** -0.5 * beta)
        inp[f"l{i}_ln3_g"] = gain(D)
        inp[f"l{i}_ln3_b"] = nrm((D,), 0.02)

    def moe(i):
        inp[f"l{i}_moe_router"] = nrm((D, N_EXPERTS), D ** -0.5)
        inp[f"l{i}_moe_bias"] = nrm((N_EXPERTS,), 0.01)
        inp[f"l{i}_moe_w_in"] = nrm((N_EXPERTS, D, 2 * D_FF_EXPERT), D ** -0.5)
        inp[f"l{i}_moe_w_out"] = nrm((N_EXPERTS, D_FF_EXPERT, D), D_FF_EXPERT ** -0.5 * beta)
        inp[f"l{i}_ln3_g"] = gain(D)
        inp[f"l{i}_ln3_b"] = nrm((D,), 0.02)

    G = NSA_KV_GROUPS
    inp["l0_nsa_w_in"] = nrm((D, H * dh + 6 * G * dh + 3 * H), D ** -0.5)
    inp["l0_nsa_cmp_pe"] = nrm((2, NSA_CMP_BLOCK, dh), 0.1)
    inp["l0_nsa_cmp_w1"] = nrm((2, NSA_CMP_BLOCK * dh, NSA_CMP_HIDDEN), (NSA_CMP_BLOCK * dh) ** -0.5)
    inp["l0_nsa_cmp_w2"] = nrm((2, NSA_CMP_HIDDEN, dh), NSA_CMP_HIDDEN ** -0.5)
    inp["l0_nsa_w_out"] = nrm((H * dh, D), (H * dh) ** -0.5 * beta)
    common_attn(0)
    dense_ffn(0)

    inp["l1_mla_w_down"] = nrm((D, MLA_Q_RANK + MLA_KV_RANK + MLA_ROPE_DIM), D ** -0.5)
    inp["l1_mla_q_norm"] = gain(MLA_Q_RANK)
    inp["l1_mla_kv_norm"] = gain(MLA_KV_RANK)
    inp["l1_mla_w_uq"] = nrm((MLA_Q_RANK, H * (MLA_NOPE_DIM + MLA_ROPE_DIM)), MLA_Q_RANK ** -0.5)
    inp["l1_mla_w_ukv"] = nrm((MLA_KV_RANK, H * (MLA_NOPE_DIM + MLA_V_DIM)), MLA_KV_RANK ** -0.5)
    inp["l1_mla_w_out"] = nrm((H * MLA_V_DIM, D), (H * MLA_V_DIM) ** -0.5 * beta)
    common_attn(1)
    moe(1)

    inp["l2_moba_w_in"] = nrm((D, 3 * H * dh), D ** -0.5)
    inp["l2_moba_w_out"] = nrm((H * dh, D), (H * dh) ** -0.5 * beta)
    common_attn(2)
    dense_ffn(2)

    inp["l3_swa_w_in"] = nrm((D, H * dh + 2 * SWA_KV_HEADS * dh), D ** -0.5)
    inp["l3_swa_sinks"] = nrm((H,), 1.0)
    inp["l3_swa_w_out"] = nrm((H * dh, D), (H * dh) ** -0.5 * beta)
    common_attn(3)
    moe(3)
    return inp


def reference(x, mem, positions,
              l0_nsa_w_in, l0_nsa_cmp_pe, l0_nsa_cmp_w1, l0_nsa_cmp_w2, l0_nsa_w_out,
              l0_ln1_g, l0_ln1_b, l0_xq, l0_xkv, l0_xo, l0_ln2_g, l0_ln2_b,
              l0_ffn_w_in, l0_ffn_w_out, l0_ln3_g, l0_ln3_b,
              l1_mla_w_down, l1_mla_q_norm, l1_mla_kv_norm, l1_mla_w_uq, l1_mla_w_ukv, l1_mla_w_out,
              l1_ln1_g, l1_ln1_b, l1_xq, l1_xkv, l1_xo, l1_ln2_g, l1_ln2_b,
              l1_moe_router, l1_moe_bias, l1_moe_w_in, l1_moe_w_out, l1_ln3_g, l1_ln3_b,
              l2_moba_w_in, l2_moba_w_out,
              l2_ln1_g, l2_ln1_b, l2_xq, l2_xkv, l2_xo, l2_ln2_g, l2_ln2_b,
              l2_ffn_w_in, l2_ffn_w_out, l2_ln3_g, l2_ln3_b,
              l3_swa_w_in, l3_swa_sinks, l3_swa_w_out,
              l3_ln1_g, l3_ln1_b, l3_xq, l3_xkv, l3_xo, l3_ln2_g, l3_ln2_b,
              l3_moe_router, l3_moe_bias, l3_moe_w_in, l3_moe_w_out, l3_ln3_g, l3_ln3_b):
    cos_p, sin_p = rope_tables(positions, ROPE_DIM)
    cos_m, sin_m = rope_tables(positions, MLA_ROPE_DIM)

    mixers = [
        lambda h: nsa_mixer(h, cos_p, sin_p, l0_nsa_w_in, l0_nsa_cmp_pe, l0_nsa_cmp_w1, l0_nsa_cmp_w2, l0_nsa_w_out),
        lambda h: mla_mixer(h, cos_m, sin_m, l1_mla_w_down, l1_mla_q_norm, l1_mla_kv_norm,
                            l1_mla_w_uq, l1_mla_w_ukv, l1_mla_w_out),
        lambda h: moba_mixer(h, cos_p, sin_p, l2_moba_w_in, l2_moba_w_out),
        lambda h: swa_mixer(h, cos_p, sin_p, l3_swa_w_in, l3_swa_sinks, l3_swa_w_out),
    ]
    ln1 = [(l0_ln1_g, l0_ln1_b), (l1_ln1_g, l1_ln1_b), (l2_ln1_g, l2_ln1_b), (l3_ln1_g, l3_ln1_b)]
    xattn = [(l0_xq, l0_xkv, l0_xo), (l1_xq, l1_xkv, l1_xo), (l2_xq, l2_xkv, l2_xo), (l3_xq, l3_xkv, l3_xo)]
    ln2 = [(l0_ln2_g, l0_ln2_b), (l1_ln2_g, l1_ln2_b), (l2_ln2_g, l2_ln2_b), (l3_ln2_g, l3_ln2_b)]
    ffns = [
        lambda h: swiglu(h, l0_ffn_w_in, l0_ffn_w_out),
        lambda h: moe_ffn(h, l1_moe_router, l1_moe_bias, l1_moe_w_in, l1_moe_w_out),
        lambda h: swiglu(h, l2_ffn_w_in, l2_ffn_w_out),
        lambda h: moe_ffn(h, l3_moe_router, l3_moe_bias, l3_moe_w_in, l3_moe_w_out),
    ]
    ln3 = [(l0_ln3_g, l0_ln3_b), (l1_ln3_g, l1_ln3_b), (l2_ln3_g, l2_ln3_b), (l3_ln3_g, l3_ln3_b)]

    h = x
    for i in range(DEPTH):
        h = layer_norm(DEEPNORM_ALPHA * h + mixers[i % N_MIXERS](h), *ln1[i])
        h = layer_norm(DEEPNORM_ALPHA * h + cross_attention(h, mem, *xattn[i]), *ln2[i])
        h = layer_norm(DEEPNORM_ALPHA * h + ffns[i](h), *ln3[i])
    return h
```

```python
import functools

import numpy as np
import jax
import jax.numpy as jnp
from jax import lax
from jax.experimental import pallas as pl
from jax.experimental.pallas import tpu as pltpu

BF16 = jnp.bfloat16
F32 = jnp.float32

D_MODEL = 1024
HEAD_DIM = 64
N_HEADS = D_MODEL // HEAD_DIM
ROPE_THETA = 500000.0
ROPE_DIM = HEAD_DIM // 4

NSA_KV_GROUPS = 4
NSA_CMP_BLOCK = 32
NSA_CMP_STRIDE = 16
NSA_SLC_BLOCK = 64
NSA_SLC_TOPN = 8
NSA_WINDOW = 512
NSA_FORCE_SCORE = 1e4

MLA_Q_RANK = D_MODEL // 4
MLA_KV_RANK = D_MODEL // 8
MLA_NOPE_DIM = 64
MLA_ROPE_DIM = 32
MLA_V_DIM = 64

MOBA_BLOCK = 256
MOBA_TOPK = 3

SWA_KV_HEADS = 2
SWA_WINDOW = 128

XATTN_HEADS = 4
XATTN_HEAD_DIM = D_MODEL // XATTN_HEADS

D_FF = (D_MODEL * 7) // 2
N_EXPERTS = 8
TOP_K = 2

DEPTH = 4
DEEPNORM_ALPHA = (2.0 * DEPTH) ** 0.25
LN_EPS = 1e-5
RMS_EPS = 1e-6

LANES = 128
NEG_INF = float("-inf")
M_INIT = -1e30
VMEM_LIMIT = 48 * 1024 * 1024


def _params(*sem):
    return pltpu.CompilerParams(dimension_semantics=sem, vmem_limit_bytes=VMEM_LIMIT)


def _ln(z, g, b):
    mu = jnp.mean(z, axis=-1, keepdims=True)
    zc = z - mu
    var = jnp.mean(zc * zc, axis=-1, keepdims=True)
    return zc * lax.rsqrt(var + LN_EPS) * g + b


def _proj_kernel(*refs, n_out, rope, shift, rms, chunk):
    x_ref = refs[0]
    pos = 1
    if rms:
        g_ref = refs[pos]
        pos += 1
    w_refs = refs[pos:pos + n_out]
    pos += n_out
    if any(rope):
        c_ref, s1_ref, s2_ref = refs[pos:pos + 3]
        pos += 3
    o_refs = refs[pos:pos + n_out]

    x = x_ref[...]
    if rms:
        xf = x.astype(F32)
        xf = xf * lax.rsqrt(jnp.mean(xf * xf, axis=-1, keepdims=True) + RMS_EPS) * g_ref[...]
        xb = xf.astype(BF16)
    else:
        xb = x.astype(BF16)
    for i in range(n_out):
        n = w_refs[i].shape[1]
        for c0 in range(0, n, chunk):
            cw = min(chunk, n - c0)
            acc = jnp.dot(xb, w_refs[i][:, c0:c0 + cw], preferred_element_type=F32)
            if rope[i]:
                reps = cw // LANES
                c, s1, s2 = c_ref[...], s1_ref[...], s2_ref[...]
                if reps > 1:
                    c = jnp.concatenate([c] * reps, axis=1)
                    s1 = jnp.concatenate([s1] * reps, axis=1)
                    s2 = jnp.concatenate([s2] * reps, axis=1)
                acc = acc * c + pltpu.roll(acc, shift, 1) * s1 + pltpu.roll(acc, cw - shift, 1) * s2
            o_refs[i][:, c0:c0 + cw] = acc.astype(o_refs[i].dtype)


def _proj(x, ws, out_dtypes, *, rope=None, tables=None, shift=0, rms_g=None, tm=512):
    m, k = x.shape
    n_out = len(ws)
    rope = tuple(rope) if rope is not None else (False,) * n_out
    tm = min(tm, m)
    assert m % tm == 0
    in_specs = [pl.BlockSpec((tm, k), lambda i: (i, 0))]
    args = [x]
    if rms_g is not None:
        in_specs.append(pl.BlockSpec((1, k), lambda i: (0, 0)))
        args.append(rms_g.reshape(1, k).astype(F32))
    for w in ws:
        assert w.shape[1] % LANES == 0
        in_specs.append(pl.BlockSpec(w.shape, lambda i: (0, 0)))
        args.append(w)
    if any(rope):
        for t in tables:
            in_specs.append(pl.BlockSpec((tm, LANES), lambda i: (i, 0)))
            args.append(t)
    out_shape = [jax.ShapeDtypeStruct((m, w.shape[1]), dt) for w, dt in zip(ws, out_dtypes)]
    out_specs = [pl.BlockSpec((tm, w.shape[1]), lambda i: (i, 0)) for w in ws]
    kern = functools.partial(_proj_kernel, n_out=n_out, rope=rope, shift=shift,
                             rms=rms_g is not None, chunk=2 * LANES)
    return pl.pallas_call(kern, out_shape=out_shape, grid=(m // tm,), in_specs=in_specs,
                          out_specs=out_specs, compiler_params=_params("parallel"), name="proj")(*args)


def _rope_tables(positions, rot_dim, period):
    half = rot_dim // 2
    inv = ROPE_THETA ** (-jnp.arange(0, rot_dim, 2, dtype=F32) / rot_dim)
    ang = positions.astype(F32).reshape(-1)[:, None] * inv
    cos, sin = jnp.cos(ang), jnp.sin(ang)
    t = ang.shape[0]
    zero_h = jnp.zeros((t, half), F32)
    rest = period - rot_dim
    c = jnp.concatenate([cos, cos, jnp.ones((t, rest), F32)], axis=1)
    s1 = jnp.concatenate([zero_h, sin, jnp.zeros((t, rest), F32)], axis=1)
    s2 = jnp.concatenate([-sin, zero_h, jnp.zeros((t, rest), F32)], axis=1)
    reps = LANES // period
    return tuple(jnp.tile(a, (1, reps)) for a in (c, s1, s2))


def _mm_res_ln_kernel(a_ref, w_ref, h_ref, g_ref, b_ref, o_ref):
    y = jnp.dot(a_ref[...].astype(BF16), w_ref[...], preferred_element_type=F32)
    o_ref[...] = _ln(DEEPNORM_ALPHA * h_ref[...] + y, g_ref[...], b_ref[...])


def _mm_res_ln(a, w, h, g, b, tm=512):
    m, k = a.shape
    d = w.shape[1]
    tm = min(tm, m)
    return pl.pallas_call(
        _mm_res_ln_kernel,
        out_shape=jax.ShapeDtypeStruct((m, d), F32),
        grid=(m // tm,),
        in_specs=[pl.BlockSpec((tm, k), lambda i: (i, 0)),
                  pl.BlockSpec((k, d), lambda i: (0, 0)),
                  pl.BlockSpec((tm, d), lambda i: (i, 0)),
                  pl.BlockSpec((1, d), lambda i: (0, 0)),
                  pl.BlockSpec((1, d), lambda i: (0, 0))],
        out_specs=pl.BlockSpec((tm, d), lambda i: (i, 0)),
        compiler_params=_params("parallel"), name="mm_res_ln",
    )(a, w, h, g.reshape(1, d), b.reshape(1, d))


def _add_ln_kernel(h_ref, a_ref, b2_ref, g_ref, b_ref, o_ref):
    z = DEEPNORM_ALPHA * h_ref[...] + (a_ref[...].astype(F32) + b2_ref[...].astype(F32))
    o_ref[...] = _ln(z, g_ref[...], b_ref[...])


def _add_ln(h, a, b2, g, b, tm=512):
    m, d = h.shape
    tm = min(tm, m)
    row = pl.BlockSpec((tm, d), lambda i: (i, 0))
    vec = pl.BlockSpec((1, d), lambda i: (0, 0))
    return pl.pallas_call(
        _add_ln_kernel, out_shape=jax.ShapeDtypeStruct((m, d), F32), grid=(m // tm,),
        in_specs=[row, row, row, vec, vec], out_specs=row,
        compiler_params=_params("parallel"), name="add_ln",
    )(h, a, b2, g.reshape(1, d), b.reshape(1, d))


def _flash(q, k_ref, v_ref, row_pos, lo, hi, *, tk, scale, window=None, m0=None, l0=None,
           bm_ref=None, nrep=1):
    rows = q.shape[0]
    dv = v_ref.shape[-1]
    if m0 is None:
        m0 = jnp.full((rows, 1), M_INIT, F32)
        l0 = jnp.zeros((rows, 1), F32)
    acc0 = jnp.zeros((rows, dv), F32)

    def body(j, carry):
        m, l, acc = carry
        start = pl.multiple_of(j * tk, tk)
        k = k_ref[pl.ds(start, tk), :]
        v = v_ref[pl.ds(start, tk), :]
        s = lax.dot_general(q, k, (((1,), (1,)), ((), ())), preferred_element_type=F32) * scale
        kpos = start + lax.broadcasted_iota(jnp.int32, (1, tk), 1)
        delta = row_pos - kpos
        mask = delta >= 0
        if window is not None:
            mask = mask & (delta < window)
        if bm_ref is not None:
            bm = bm_ref[j]
            if nrep > 1:
                bm = jnp.concatenate([bm] * nrep, axis=0)
            mask = mask & (bm > 0.5)
        s = jnp.where(mask, s, NEG_INF)
        m_new = jnp.maximum(m, jnp.max(s, axis=-1, keepdims=True))
        a = jnp.exp(m - m_new)
        p = jnp.exp(s - m_new)
        l_new = a * l + jnp.sum(p, axis=-1, keepdims=True)
        acc_new = a * acc + jnp.dot(p.astype(BF16), v, preferred_element_type=F32)
        return m_new, l_new, acc_new

    m, l, acc = lax.fori_loop(lo, hi + 1, body, (m0, l0, acc0))
    return acc / jnp.maximum(l, 1e-30)


def _div_pow2(x, d):
    assert d & (d - 1) == 0
    return x >> (d.bit_length() - 1)


def _row_positions(q0, rows, tq):
    return q0 + (lax.broadcasted_iota(jnp.int32, (rows, 1), 0) & (tq - 1))


def _topk_mask(score, valid, k):
    lane = lax.broadcasted_iota(jnp.int32, score.shape, 1)
    rank = jnp.zeros(score.shape, F32)
    for j in range(score.shape[1]):
        col = score[:, j:j + 1]
        beats = (col > score) | ((col == score) & (lane > j))
        rank = rank + jnp.where(beats, 1.0, 0.0)
    return jnp.where(valid & (rank < k), 1.0, 0.0)


def _fill_blockmask(bm_ref, selm, blk):
    nblk, tq, tk = bm_ref.shape
    per = tk // blk
    lane = lax.broadcasted_iota(jnp.int32, (tq, tk), 1)
    for jb in range(nblk):
        base = jb * per
        val = jnp.broadcast_to(selm[:, base + per - 1:base + per], (tq, tk))
        for c in range(per - 2, -1, -1):
            val = jnp.where(lane < (c + 1) * blk, selm[:, base + c:base + c + 1], val)
        bm_ref[jb] = val


def _attn_kernel(*refs, rep, tq, tk, scale, window, has_sink):
    if has_sink:
        q_ref, k_ref, v_ref, sink_ref, o_ref = refs
    else:
        q_ref, k_ref, v_ref, o_ref = refs
    qi = pl.program_id(2)
    dk = q_ref.shape[-1]
    rows = rep * tq
    q = q_ref[...].reshape(rows, dk)
    q0 = qi * tq
    row_pos = _row_positions(q0, rows, tq)
    hi = _div_pow2(q0 + tq - 1, tk)
    lo = 0 if window is None else _div_pow2(jnp.maximum(q0 - (window - 1), 0), tk)
    m0 = l0 = None
    if has_sink:
        m0 = sink_ref[...]
        l0 = jnp.ones((rows, 1), F32)
    o = _flash(q, k_ref, v_ref, row_pos, lo, hi, tk=tk, scale=scale, window=window, m0=m0, l0=l0)
    o_ref[...] = o.reshape(rep, tq, o.shape[-1]).astype(o_ref.dtype)


def _attention(q, k, v, *, scale, window=None, sink_rows=None, tq=128, tk=128):
    b, g, rep, s, dk = q.shape
    dv = v.shape[-1]
    tq, tk = min(tq, s), min(tk, s)
    in_specs = [pl.BlockSpec((None, None, rep, tq, dk), lambda bi, gi, i: (bi, gi, 0, i, 0)),
                pl.BlockSpec((None, None, s, dk), lambda bi, gi, i: (bi, gi, 0, 0)),
                pl.BlockSpec((None, None, s, dv), lambda bi, gi, i: (bi, gi, 0, 0))]
    args = [q, k, v]
    if sink_rows is not None:
        in_specs.append(pl.BlockSpec((None, rep * tq, 1), lambda bi, gi, i: (gi, 0, 0)))
        args.append(sink_rows)
    kern = functools.partial(_attn_kernel, rep=rep, tq=tq, tk=tk, scale=scale, window=window,
                             has_sink=sink_rows is not None)
    return pl.pallas_call(
        kern, out_shape=jax.ShapeDtypeStruct((b, g, rep, s, dv), BF16), grid=(b, g, s // tq),
        in_specs=in_specs,
        out_specs=pl.BlockSpec((None, None, rep, tq, dv), lambda bi, gi, i: (bi, gi, 0, i, 0)),
        compiler_params=_params("parallel", "parallel", "parallel"), name="attention",
    )(*args)


def _gelu_tanh(x):
    return 0.5 * x * (1.0 + jnp.tanh(np.sqrt(2.0 / np.pi).astype(np.float32) * (x + 0.044715 * (x * x * x))))


def _nsa_compress_kernel(ak_ref, av_ref, pe_ref, w1_ref, w2_ref, kc_ref, vc_ref):
    half = ak_ref.shape[-1]
    ncp = ak_ref.shape[0]
    for j, (a_ref, o_ref) in enumerate(((ak_ref, kc_ref), (av_ref, vc_ref))):
        a = a_ref[...].astype(F32)
        lo = (a + pe_ref[j, 0:1, :]).astype(BF16)
        hi = (a + pe_ref[j, 1:2, :]).astype(BF16)
        h1 = jnp.dot(lo, w1_ref[j, :half, :], preferred_element_type=F32)
        h2 = jnp.dot(hi, w1_ref[j, half:, :], preferred_element_type=F32)
        h = _gelu_tanh(h1 + pltpu.roll(h2, ncp - 1, 0))
        o_ref[...] = jnp.dot(h.astype(BF16), w2_ref[j], preferred_element_type=F32).astype(o_ref.dtype)


def _nsa_compress(ak, av, pe2, w1, w2):
    b, g, ncp, wid = ak.shape
    dh = w2.shape[-1]
    blk = pl.BlockSpec((None, None, ncp, wid), lambda bi, gi: (bi, gi, 0, 0))
    full = lambda a: pl.BlockSpec(a.shape, lambda bi, gi: (0,) * a.ndim)
    out = pl.BlockSpec((None, None, ncp, dh), lambda bi, gi: (bi, gi, 0, 0))
    return pl.pallas_call(
        _nsa_compress_kernel,
        out_shape=[jax.ShapeDtypeStruct((b, g, ncp, dh), BF16)] * 2,
        grid=(b, g), in_specs=[blk, blk, full(pe2), full(w1), full(w2)], out_specs=[out, out],
        compiler_params=_params("parallel", "parallel"), name="nsa_compress",
    )(ak, av, pe2, w1, w2)


def _nsa_attn_kernel(q_ref, kc_ref, vc_ref, ks_ref, vs_ref, kw_ref, vw_ref, gl_ref, ovl_ref, o_ref,
                     bm_ref, *, rep, tq, tk, nc, nb, n_sel, scale):
    qi = pl.program_id(2)
    dh = q_ref.shape[-1]
    rows = rep * tq
    q = q_ref[...].reshape(rows, dh)
    q0 = qi * tq
    row_pos = _row_positions(q0, rows, tq)

    ncp = kc_ref.shape[0]
    s = lax.dot_general(q, kc_ref[...], (((1,), (1,)), ((), ())), preferred_element_type=F32) * scale
    cidx = lax.broadcasted_iota(jnp.int32, (1, ncp), 1)
    cmask = (cidx * NSA_CMP_STRIDE + (NSA_CMP_BLOCK - 1) <= row_pos) & (cidx < nc)
    s = jnp.where(cmask, s, NEG_INF)
    m = jnp.max(s, axis=-1, keepdims=True)
    m = jnp.where(m == NEG_INF, 0.0, m)
    e = jnp.exp(s - m)
    p = e / jnp.maximum(jnp.sum(e, axis=-1, keepdims=True), 1e-30)
    o_cmp = jnp.dot(p.astype(BF16), vc_ref[...], preferred_element_type=F32)

    psum = p[0:tq]
    for r in range(1, rep):
        psum = psum + p[r * tq:(r + 1) * tq]
    p_hi = psum.astype(BF16)
    p_lo = (psum - p_hi.astype(F32)).astype(BF16)
    ovl = ovl_ref[...]
    imp = (jnp.dot(p_hi, ovl, preferred_element_type=F32)
           + jnp.dot(p_lo, ovl, preferred_element_type=F32))[:, :nb]
    blk = lax.broadcasted_iota(jnp.int32, (tq, nb), 1)
    cur = _div_pow2(q0 + lax.broadcasted_iota(jnp.int32, (tq, nb), 0), NSA_SLC_BLOCK)
    forced = (blk == 0) | (blk == cur) | (blk == cur - 1)
    valid = blk <= cur
    score = jnp.where(valid, jnp.where(forced, NSA_FORCE_SCORE, imp), NEG_INF)
    selm = _topk_mask(score, valid, n_sel)
    _fill_blockmask(bm_ref, selm, NSA_SLC_BLOCK)

    hi = _div_pow2(q0 + tq - 1, tk)
    o_slc = _flash(q, ks_ref, vs_ref, row_pos, 0, hi, tk=tk, scale=scale, bm_ref=bm_ref, nrep=rep)
    lo = _div_pow2(jnp.maximum(q0 - (NSA_WINDOW - 1), 0), tk)
    o_win = _flash(q, kw_ref, vw_ref, row_pos, lo, hi, tk=tk, scale=scale, window=NSA_WINDOW)

    gate = jax.nn.sigmoid(gl_ref[...].reshape(rows, gl_ref.shape[-1]))
    o = gate[:, 0:1] * o_cmp + gate[:, 1:2] * o_slc + gate[:, 2:3] * o_win
    o_ref[...] = o.reshape(rep, tq, dh).astype(o_ref.dtype)


def _nsa_attention(q, kc, vc, ks, vs, kw, vw, gl, ovl, *, nc, tq=128, tk=128):
    b, g, rep, s, dh = q.shape
    ncp = kc.shape[2]
    nb = s // NSA_SLC_BLOCK
    tq, tk = min(tq, s), min(tk, s)
    qspec = pl.BlockSpec((None, None, rep, tq, dh), lambda bi, gi, i: (bi, gi, 0, i, 0))
    cspec = pl.BlockSpec((None, None, ncp, dh), lambda bi, gi, i: (bi, gi, 0, 0))
    kspec = pl.BlockSpec((None, None, s, dh), lambda bi, gi, i: (bi, gi, 0, 0))
    gspec = pl.BlockSpec((None, None, rep, tq, gl.shape[-1]), lambda bi, gi, i: (bi, gi, 0, i, 0))
    ospec = pl.BlockSpec(ovl.shape, lambda bi, gi, i: (0, 0))
    kern = functools.partial(_nsa_attn_kernel, rep=rep, tq=tq, tk=tk, nc=nc, nb=nb,
                             n_sel=min(NSA_SLC_TOPN, nb), scale=dh ** -0.5)
    return pl.pallas_call(
        kern, out_shape=jax.ShapeDtypeStruct((b, g, rep, s, dh), BF16), grid=(b, g, s // tq),
        in_specs=[qspec, cspec, cspec, kspec, kspec, kspec, kspec, gspec, ospec], out_specs=qspec,
        scratch_shapes=[pltpu.VMEM((s // tk, tq, tk), F32)],
        compiler_params=_params("parallel", "parallel", "parallel"), name="nsa_attention",
    )(q, kc, vc, ks, vs, kw, vw, gl, ovl)


def _heads_first(t, b, s, h):
    return t.reshape(b, s, h, t.shape[-1] // h).transpose(0, 2, 1, 3)


def _heads_last(o, b, s):
    return o.transpose(0, 2, 1, 3).reshape(b * s, -1)


def _nsa_mixer(h, b, s, tab_p, w_in, cmp_pe, cmp_w1, cmp_w2):
    hn, g, dh = N_HEADS, NSA_KV_GROUPS, HEAD_DIM
    rep = hn // g
    gd = g * dh
    q_end = hn * dh
    kv = lambda j: w_in[:, q_end + j * gd:q_end + (j + 1) * gd]
    w_rope = jnp.concatenate([w_in[:, :q_end], kv(0), kv(2), kv(4)], axis=1).astype(BF16)
    w_plain = jnp.concatenate([kv(1), kv(3), kv(5)], axis=1).astype(BF16)
    n_gate = 3 * hn
    w_gate = jnp.pad(w_in[:, q_end + 6 * gd:], ((0, 0), (0, LANES - n_gate))).astype(BF16)
    roped, plain, gates = _proj(h, [w_rope, w_plain, w_gate], [BF16, BF16, F32],
                                rope=[True, False, False], tables=tab_p, shift=ROPE_DIM // 2)
    q = _heads_first(roped[:, :q_end], b, s, hn).reshape(b, g, rep, s, dh)
    k_c, k_s, k_w = [_heads_first(roped[:, q_end + j * gd:q_end + (j + 1) * gd], b, s, g) for j in range(3)]
    v_c, v_s, v_w = [_heads_first(plain[:, j * gd:(j + 1) * gd], b, s, g) for j in range(3)]
    gl = gates[:, :n_gate].reshape(b, s, 3, g, rep).transpose(0, 3, 4, 1, 2)

    nc = (s - NSA_CMP_BLOCK) // NSA_CMP_STRIDE + 1
    ncp = s // NSA_CMP_STRIDE
    half = NSA_CMP_STRIDE * dh
    ak = k_c.reshape(b, g, ncp, half)
    av = v_c.reshape(b, g, ncp, half)
    pe2 = cmp_pe.reshape(2, 2, half).astype(F32)
    kc, vc = _nsa_compress(ak, av, pe2, cmp_w1.astype(BF16), cmp_w2.astype(BF16))

    nb = s // NSA_SLC_BLOCK
    c_start = np.arange(ncp) * NSA_CMP_STRIDE
    b_start = np.arange(nb) * NSA_SLC_BLOCK
    overlap = np.clip(np.minimum(c_start[:, None] + NSA_CMP_BLOCK, b_start[None, :] + NSA_SLC_BLOCK)
                      - np.maximum(c_start[:, None], b_start[None, :]), 0, None) / NSA_CMP_BLOCK
    overlap[nc:] = 0.0
    ovl = jnp.asarray(np.pad(overlap, ((0, 0), (0, LANES - nb))), BF16)
    o = _nsa_attention(q, kc, vc, k_s, v_s, k_w, v_w, gl, ovl, nc=nc)
    return _heads_last(o.reshape(b, hn, s, dh), b, s)


def _mla_mixer(h, b, s, tab_m, w_down, q_norm, kv_norm, w_uq, w_ukv):
    hn = N_HEADS
    qr_, kvr = MLA_Q_RANK, MLA_KV_RANK
    w_kr = jnp.pad(w_down[:, qr_ + kvr:], ((0, 0), (0, LANES - MLA_ROPE_DIM)))
    cq, ckv, kr = _proj(h, [w_down[:, :qr_].astype(BF16), w_down[:, qr_:qr_ + kvr].astype(BF16),
                            w_kr.astype(BF16)], [F32, F32, BF16],
                        rope=[False, False, True], tables=tab_m, shift=MLA_ROPE_DIM // 2)
    w_uq3 = w_uq.reshape(qr_, hn, MLA_NOPE_DIM + MLA_ROPE_DIM)
    w_qn = w_uq3[:, :, :MLA_NOPE_DIM].reshape(qr_, hn * MLA_NOPE_DIM).astype(BF16)
    w_qr = w_uq3[:, :, MLA_NOPE_DIM:].reshape(qr_, hn * MLA_ROPE_DIM).astype(BF16)
    qn, qr = _proj(cq, [w_qn, w_qr], [BF16, BF16], rope=[False, True], tables=tab_m,
                   shift=MLA_ROPE_DIM // 2, rms_g=q_norm)
    w_ukv3 = w_ukv.reshape(kvr, hn, MLA_NOPE_DIM + MLA_V_DIM)
    w_kn = w_ukv3[:, :, :MLA_NOPE_DIM].reshape(kvr, hn * MLA_NOPE_DIM).astype(BF16)
    w_v = w_ukv3[:, :, MLA_NOPE_DIM:].reshape(kvr, hn * MLA_V_DIM).astype(BF16)
    kn, v = _proj(ckv, [w_kn, w_v], [BF16, BF16], rms_g=kv_norm)

    dpad = LANES - MLA_NOPE_DIM - MLA_ROPE_DIM
    zpad = jnp.zeros((b, hn, s, dpad), BF16)
    q = jnp.concatenate([_heads_first(qn, b, s, hn), _heads_first(qr, b, s, hn), zpad], axis=-1)
    k_rope = jnp.broadcast_to(kr[:, :MLA_ROPE_DIM].reshape(b, 1, s, MLA_ROPE_DIM), (b, hn, s, MLA_ROPE_DIM))
    k = jnp.concatenate([_heads_first(kn, b, s, hn), k_rope, zpad], axis=-1)
    vv = _heads_first(v, b, s, hn)
    scale = (MLA_NOPE_DIM + MLA_ROPE_DIM) ** -0.5
    o = _attention(q[:, :, None], k, vv, scale=scale, tq=256, tk=256)[:, :, 0]
    return _heads_last(o, b, s)


def _moba_kernel(q_ref, k_ref, v_ref, o_ref, bm_ref, *, tq, n_sel, scale):
    qi = pl.program_id(2)
    s_len, dh = k_ref.shape
    nb = s_len // MOBA_BLOCK
    q = q_ref[...]
    q0 = qi * tq
    row_pos = _row_positions(q0, tq, tq)

    qf = q.astype(F32)
    blk = lax.broadcasted_iota(jnp.int32, (tq, nb), 1)
    gate = jnp.zeros((tq, nb), F32)
    for j in range(nb):
        kj = k_ref[j * MOBA_BLOCK:(j + 1) * MOBA_BLOCK, :].astype(F32)
        k_mean = jnp.sum(kj, axis=0, keepdims=True) * (1.0 / MOBA_BLOCK)
        gate = jnp.where(blk == j, jnp.sum(qf * k_mean, axis=-1, keepdims=True), gate)
    own = _div_pow2(q0 + lax.broadcasted_iota(jnp.int32, (tq, nb), 0), MOBA_BLOCK)
    valid = blk < own
    score = jnp.where(valid, gate, NEG_INF)
    selm = jnp.maximum(_topk_mask(score, valid, n_sel), jnp.where(blk == own, 1.0, 0.0))
    _fill_blockmask(bm_ref, selm, MOBA_BLOCK)
    hi = _div_pow2(q0 + tq - 1, MOBA_BLOCK)
    o = _flash(q, k_ref, v_ref, row_pos, 0, hi, tk=MOBA_BLOCK, scale=scale, bm_ref=bm_ref)
    o_ref[...] = o.astype(o_ref.dtype)


def _moba_attention(q, k, v, tq=256):
    b, hn, s, dh = q.shape
    nb = s // MOBA_BLOCK
    n_sel = max(1, min(MOBA_TOPK, nb - 1))
    tq = min(tq, s)
    qspec = pl.BlockSpec((None, None, tq, dh), lambda bi, hi, i: (bi, hi, i, 0))
    kspec = pl.BlockSpec((None, None, s, dh), lambda bi, hi, i: (bi, hi, 0, 0))
    kern = functools.partial(_moba_kernel, tq=tq, n_sel=n_sel, scale=dh ** -0.5)
    return pl.pallas_call(
        kern, out_shape=jax.ShapeDtypeStruct((b, hn, s, dh), BF16), grid=(b, hn, s // tq),
        in_specs=[qspec, kspec, kspec], out_specs=qspec,
        scratch_shapes=[pltpu.VMEM((nb, tq, MOBA_BLOCK), F32)],
        compiler_params=_params("parallel", "parallel", "parallel"), name="moba_attention",
    )(q, k, v)


def _moba_mixer(h, b, s, tab_p, w_in):
    hn, dh = N_HEADS, HEAD_DIM
    assert s % MOBA_BLOCK == 0
    qk, v = _proj(h, [w_in[:, :2 * hn * dh].astype(BF16), w_in[:, 2 * hn * dh:].astype(BF16)], [BF16, BF16],
                  rope=[True, False], tables=tab_p, shift=ROPE_DIM // 2)
    q = _heads_first(qk[:, :hn * dh], b, s, hn)
    k = _heads_first(qk[:, hn * dh:], b, s, hn)
    o = _moba_attention(q, k, _heads_first(v, b, s, hn))
    return _heads_last(o, b, s)


def _swa_mixer(h, b, s, tab_p, w_in, sinks):
    hn, g, dh = N_HEADS, SWA_KV_HEADS, HEAD_DIM
    rep = hn // g
    tq = min(128, s)
    qk, v = _proj(h, [w_in[:, :(hn + g) * dh].astype(BF16), w_in[:, (hn + g) * dh:].astype(BF16)], [BF16, BF16],
                  rope=[True, False], tables=tab_p, shift=ROPE_DIM // 2)
    q = _heads_first(qk[:, :hn * dh], b, s, hn).reshape(b, g, rep, s, dh)
    k = _heads_first(qk[:, hn * dh:], b, s, g)
    vv = _heads_first(v, b, s, g)
    sink_rows = jnp.repeat(sinks.astype(F32).reshape(g, rep), tq, axis=1).reshape(g, rep * tq, 1)
    o = _attention(q, k, vv, scale=dh ** -0.5, window=SWA_WINDOW, sink_rows=sink_rows, tq=tq, tk=tq)
    return _heads_last(o.reshape(b, hn, s, dh), b, s)


def _xattn_kernel(q_ref, kv_ref, o_ref, *, scale):
    d = q_ref.shape[-1]
    hd = XATTN_HEAD_DIM
    for hh in range(XATTN_HEADS):
        q = q_ref[:, hh * hd:(hh + 1) * hd]
        k = kv_ref[:, hh * hd:(hh + 1) * hd]
        v = kv_ref[:, d + hh * hd:d + (hh + 1) * hd]
        s = lax.dot_general(q, k, (((1,), (1,)), ((), ())), preferred_element_type=F32) * scale
        m = jnp.max(s, axis=-1, keepdims=True)
        e = jnp.exp(s - m)
        p = e / jnp.sum(e, axis=-1, keepdims=True)
        o_ref[:, hh * hd:(hh + 1) * hd] = jnp.dot(p.astype(BF16), v, preferred_element_type=F32).astype(o_ref.dtype)


def _cross_attention(q, kv, b, s, tq=512):
    d = q.shape[-1]
    mlen = kv.shape[0] // b
    tq = min(tq, s)
    q3 = q.reshape(b, s, d)
    kv3 = kv.reshape(b, mlen, 2 * d)
    o = pl.pallas_call(
        functools.partial(_xattn_kernel, scale=XATTN_HEAD_DIM ** -0.5),
        out_shape=jax.ShapeDtypeStruct((b, s, d), BF16), grid=(b, s // tq),
        in_specs=[pl.BlockSpec((None, tq, d), lambda bi, i: (bi, i, 0)),
                  pl.BlockSpec((None, mlen, 2 * d), lambda bi, i: (bi, 0, 0))],
        out_specs=pl.BlockSpec((None, tq, d), lambda bi, i: (bi, i, 0)),
        compiler_params=_params("parallel", "parallel"), name="cross_attention",
    )(q3, kv3)
    return o.reshape(b * s, d)


def _swiglu_partial(xb, wg_ref, wu_ref, wo_ref):
    gte = jnp.dot(xb, wg_ref[...], preferred_element_type=F32)
    up = jnp.dot(xb, wu_ref[...], preferred_element_type=F32)
    act = gte * jax.nn.sigmoid(gte) * up
    return jnp.dot(act.astype(BF16), wo_ref[...], preferred_element_type=F32)


def _ffn_ln_kernel(x_ref, wg_ref, wu_ref, wo_ref, g_ref, b_ref, o_ref, acc_ref):
    f = pl.program_id(1)

    @pl.when(f == 0)
    def _():
        acc_ref[...] = jnp.zeros_like(acc_ref)

    acc_ref[...] += _swiglu_partial(x_ref[...].astype(BF16), wg_ref, wu_ref, wo_ref)

    @pl.when(f == pl.num_programs(1) - 1)
    def _():
        o_ref[...] = _ln(DEEPNORM_ALPHA * x_ref[...] + acc_ref[...], g_ref[...], b_ref[...])


def _ffn_ln(h, w_in, w_out, g, b, tm=1024, tf=512):
    m, d = h.shape
    dff = w_out.shape[0]
    tm = min(tm, m)
    nf = dff // tf
    row = pl.BlockSpec((tm, d), lambda i, f: (i, 0))
    vec = pl.BlockSpec((1, d), lambda i, f: (0, 0))
    return pl.pallas_call(
        _ffn_ln_kernel, out_shape=jax.ShapeDtypeStruct((m, d), F32), grid=(m // tm, nf),
        in_specs=[row,
                  pl.BlockSpec((d, tf), lambda i, f: (0, f)),
                  pl.BlockSpec((d, tf), lambda i, f: (0, nf + f)),
                  pl.BlockSpec((tf, d), lambda i, f: (f, 0)),
                  vec, vec],
        out_specs=row, scratch_shapes=[pltpu.VMEM((tm, d), F32)],
        compiler_params=_params("parallel", "arbitrary"), name="ffn_ln",
    )(h, w_in.astype(BF16), w_in.astype(BF16), w_out.astype(BF16), g.reshape(1, d), b.reshape(1, d))


def _router_kernel(x_ref, rh_ref, rl_ref, b_ref, comb_ref, sel_ref):
    x = x_ref[...]
    xh = x.astype(BF16)
    xl = (x - xh.astype(F32)).astype(BF16)
    rh, rl = rh_ref[...], rl_ref[...]
    lg = (jnp.dot(xh, rh, preferred_element_type=F32) + jnp.dot(xl, rh, preferred_element_type=F32)
          + jnp.dot(xh, rl, preferred_element_type=F32)) + b_ref[...]
    lane = lax.broadcasted_iota(jnp.int32, lg.shape, 1).astype(F32)
    lg = jnp.where(lane < N_EXPERTS, lg, NEG_INF)
    m1 = jnp.max(lg, axis=-1, keepdims=True)
    i1 = jnp.min(jnp.where(lg == m1, lane, float(LANES)), axis=-1, keepdims=True)
    lg2 = jnp.where(lane == i1, NEG_INF, lg)
    m2 = jnp.max(lg2, axis=-1, keepdims=True)
    i2 = jnp.min(jnp.where(lg2 == m2, lane, float(LANES)), axis=-1, keepdims=True)
    e2 = jnp.exp(m2 - m1)
    den = 1.0 + e2
    comb_ref[...] = jnp.where(lane == i1, 1.0 / den, 0.0) + jnp.where(lane == i2, e2 / den, 0.0)
    sel_ref[...] = jnp.where((lane == i1) | (lane == i2), 1.0, 0.0)


def _router(h, router, bias, tm=512):
    m, d = h.shape
    tm = min(tm, m)
    rp = jnp.pad(router.astype(F32), ((0, 0), (0, LANES - N_EXPERTS)))
    rh = rp.astype(BF16)
    rl = (rp - rh.astype(F32)).astype(BF16)
    bp = jnp.pad(bias.astype(F32), (0, LANES - N_EXPERTS)).reshape(1, LANES)
    row = pl.BlockSpec((tm, d), lambda i: (i, 0))
    wsp = pl.BlockSpec((d, LANES), lambda i: (0, 0))
    osp = pl.BlockSpec((tm, LANES), lambda i: (i, 0))
    return pl.pallas_call(
        _router_kernel, out_shape=[jax.ShapeDtypeStruct((m, LANES), F32)] * 2, grid=(m // tm,),
        in_specs=[row, wsp, wsp, pl.BlockSpec((1, LANES), lambda i: (0, 0))], out_specs=[osp, osp],
        compiler_params=_params("parallel"), name="moe_router",
    )(h, rh, rl, bp)


def _moe_ffn_kernel(te_ref, tv_ref, x_ref, wg_ref, wu_ref, wo_ref, rs_ref, o_ref, acc_ref):
    i = pl.program_id(0)
    f = pl.program_id(1)

    @pl.when(f == 0)
    def _():
        acc_ref[...] = jnp.zeros_like(acc_ref)

    @pl.when(tv_ref[i] > 0)
    def _():
        acc_ref[...] += _swiglu_partial(x_ref[...], wg_ref, wu_ref, wo_ref)

    @pl.when(f == pl.num_programs(1) - 1)
    def _():
        o_ref[...] = acc_ref[...] * rs_ref[...]


def _moe_ffn(xs, w_in, w_out, tile_expert, tile_valid, rowscale, tm, tf=512):
    tp, d = xs.shape
    dff = w_out.shape[1]
    nf = dff // tf
    last = nf - 1

    def fblk(f, tv, i):
        return jnp.where(tv[i] > 0, f, last)

    grid_spec = pltpu.PrefetchScalarGridSpec(
        num_scalar_prefetch=2, grid=(tp // tm, nf),
        in_specs=[pl.BlockSpec((tm, d), lambda i, f, te, tv: (i, 0)),
                  pl.BlockSpec((None, d, tf), lambda i, f, te, tv: (te[i], 0, fblk(f, tv, i))),
                  pl.BlockSpec((None, d, tf), lambda i, f, te, tv: (te[i], 0, nf + fblk(f, tv, i))),
                  pl.BlockSpec((None, tf, d), lambda i, f, te, tv: (te[i], fblk(f, tv, i), 0)),
                  pl.BlockSpec((tm, 1), lambda i, f, te, tv: (i, 0))],
        out_specs=pl.BlockSpec((tm, d), lambda i, f, te, tv: (i, 0)),
        scratch_shapes=[pltpu.VMEM((tm, d), F32)])
    return pl.pallas_call(
        _moe_ffn_kernel, out_shape=jax.ShapeDtypeStruct((tp, d), F32), grid_spec=grid_spec,
        compiler_params=_params("parallel", "arbitrary"), name="moe_ffn",
    )(tile_expert, tile_valid, xs, w_in, w_in, w_out, rowscale)


def _moe_layer(h, router, bias, w_in, w_out, g, b, tm=512):
    t, d = h.shape
    tm = min(tm, t)
    comb, sel = _router(h, router, bias)
    comb, sel = comb[:, :N_EXPERTS], sel[:, :N_EXPERTS] > 0.5

    seli = sel.astype(jnp.int32)
    cnt = jnp.sum(seli, axis=0)
    ntile = (cnt + tm - 1) // tm
    tile_end = jnp.cumsum(ntile)
    row_start = (tile_end - ntile) * tm
    rank = jnp.cumsum(seli, axis=0) - seli
    n_tiles = (t * TOP_K) // tm + N_EXPERTS
    tp = n_tiles * tm
    pos = jnp.where(sel, row_start[None, :] + rank, tp)
    tiles = jnp.arange(n_tiles, dtype=jnp.int32)
    tile_valid = (tiles < tile_end[-1]).astype(jnp.int32)
    tile_expert = jnp.minimum(jnp.searchsorted(tile_end, tiles, side="right"), N_EXPERTS - 1).astype(jnp.int32)
    last_e = tile_expert[jnp.maximum(tile_end[-1] - 1, 0)]
    tile_expert = jnp.where(tile_valid > 0, tile_expert, last_e)
    tok = jnp.broadcast_to(jnp.arange(t, dtype=jnp.int32)[:, None], pos.shape)
    src = jnp.zeros((tp,), jnp.int32).at[pos.reshape(-1)].set(tok.reshape(-1), mode="drop")
    rowscale = jnp.zeros((tp,), F32).at[pos.reshape(-1)].set(comb.reshape(-1), mode="drop")
    xs = jnp.take(h.astype(BF16), src, axis=0)

    ys = _moe_ffn(xs, w_in.astype(BF16), w_out.astype(BF16), tile_expert, tile_valid,
                  rowscale.reshape(tp, 1), tm)
    p_lo = jnp.min(pos, axis=1)
    p_hi = jnp.max(jnp.where(sel, pos, -1), axis=1)
    return _add_ln(h, jnp.take(ys, p_lo, axis=0), jnp.take(ys, p_hi, axis=0), g, b)


def kernel(x, mem, positions, l0_nsa_w_in, l0_nsa_cmp_pe, l0_nsa_cmp_w1, l0_nsa_cmp_w2, l0_nsa_w_out, l0_ln1_g, l0_ln1_b, l0_xq, l0_xkv, l0_xo, l0_ln2_g, l0_ln2_b, l0_ffn_w_in, l0_ffn_w_out, l0_ln3_g, l0_ln3_b, l1_mla_w_down, l1_mla_q_norm, l1_mla_kv_norm, l1_mla_w_uq, l1_mla_w_ukv, l1_mla_w_out, l1_ln1_g, l1_ln1_b, l1_xq, l1_xkv, l1_xo, l1_ln2_g, l1_ln2_b, l1_moe_router, l1_moe_bias, l1_moe_w_in, l1_moe_w_out, l1_ln3_g, l1_ln3_b, l2_moba_w_in, l2_moba_w_out, l2_ln1_g, l2_ln1_b, l2_xq, l2_xkv, l2_xo, l2_ln2_g, l2_ln2_b, l2_ffn_w_in, l2_ffn_w_out, l2_ln3_g, l2_ln3_b, l3_swa_w_in, l3_swa_sinks, l3_swa_w_out, l3_ln1_g, l3_ln1_b, l3_xq, l3_xkv, l3_xo, l3_ln2_g, l3_ln2_b, l3_moe_router, l3_moe_bias, l3_moe_w_in, l3_moe_w_out, l3_ln3_g, l3_ln3_b):
    b, s, d = x.shape
    t = b * s
    tab_p = _rope_tables(positions, ROPE_DIM, HEAD_DIM)
    tab_m = _rope_tables(positions, MLA_ROPE_DIM, MLA_ROPE_DIM)
    mem2 = mem.reshape(-1, d)

    mixers = [
        (lambda h: _nsa_mixer(h, b, s, tab_p, l0_nsa_w_in, l0_nsa_cmp_pe, l0_nsa_cmp_w1, l0_nsa_cmp_w2), l0_nsa_w_out),
        (lambda h: _mla_mixer(h, b, s, tab_m, l1_mla_w_down, l1_mla_q_norm, l1_mla_kv_norm, l1_mla_w_uq, l1_mla_w_ukv),
         l1_mla_w_out),
        (lambda h: _moba_mixer(h, b, s, tab_p, l2_moba_w_in), l2_moba_w_out),
        (lambda h: _swa_mixer(h, b, s, tab_p, l3_swa_w_in, l3_swa_sinks), l3_swa_w_out),
    ]
    ln1 = [(l0_ln1_g, l0_ln1_b), (l1_ln1_g, l1_ln1_b), (l2_ln1_g, l2_ln1_b), (l3_ln1_g, l3_ln1_b)]
    xattn = [(l0_xq, l0_xkv, l0_xo), (l1_xq, l1_xkv, l1_xo), (l2_xq, l2_xkv, l2_xo), (l3_xq, l3_xkv, l3_xo)]
    ln2 = [(l0_ln2_g, l0_ln2_b), (l1_ln2_g, l1_ln2_b), (l2_ln2_g, l2_ln2_b), (l3_ln2_g, l3_ln2_b)]
    ln3 = [(l0_ln3_g, l0_ln3_b), (l1_ln3_g, l1_ln3_b), (l2_ln3_g, l2_ln3_b), (l3_ln3_g, l3_ln3_b)]
    ffns = [
        lambda h, g, bb: _ffn_ln(h, l0_ffn_w_in, l0_ffn_w_out, g, bb),
        lambda h, g, bb: _moe_layer(h, l1_moe_router, l1_moe_bias, l1_moe_w_in, l1_moe_w_out, g, bb),
        lambda h, g, bb: _ffn_ln(h, l2_ffn_w_in, l2_ffn_w_out, g, bb),
        lambda h, g, bb: _moe_layer(h, l3_moe_router, l3_moe_bias, l3_moe_w_in, l3_moe_w_out, g, bb),
    ]

    h = x.reshape(t, d)
    for i in range(DEPTH):
        mixer, w_out = mixers[i]
        h = _mm_res_ln(mixer(h), w_out.astype(BF16), h, *ln1[i])
        w_q, w_kv, w_o = xattn[i]
        (q,) = _proj(h, [w_q.astype(BF16)], [BF16])
        (kv,) = _proj(mem2, [w_kv.astype(BF16)], [BF16])
        h = _mm_res_ln(_cross_attention(q, kv, b, s), w_o.astype(BF16), h, *ln2[i])
        h = ffns[i](h, *ln3[i])
    return h.reshape(b, s, d)
```

```python
import functools

import numpy as np
import jax
import jax.numpy as jnp
from jax import lax
from jax.experimental import pallas as pl
from jax.experimental.pallas import tpu as pltpu

BF16 = jnp.bfloat16
F32 = jnp.float32

D_MODEL = 1024
HEAD_DIM = 64
N_HEADS = D_MODEL // HEAD_DIM
ROPE_THETA = 500000.0
ROPE_DIM = HEAD_DIM // 4

NSA_KV_GROUPS = 4
NSA_CMP_BLOCK = 32
NSA_CMP_STRIDE = 16
NSA_SLC_BLOCK = 64
NSA_SLC_TOPN = 8
NSA_WINDOW = 512
NSA_FORCE_SCORE = 1e4

MLA_Q_RANK = D_MODEL // 4
MLA_KV_RANK = D_MODEL // 8
MLA_NOPE_DIM = 64
MLA_ROPE_DIM = 32
MLA_V_DIM = 64

MOBA_BLOCK = 256
MOBA_TOPK = 3

SWA_KV_HEADS = 2
SWA_WINDOW = 128

XATTN_HEADS = 4
XATTN_HEAD_DIM = D_MODEL // XATTN_HEADS

D_FF = (D_MODEL * 7) // 2
N_EXPERTS = 8
TOP_K = 2

DEPTH = 4
DEEPNORM_ALPHA = (2.0 * DEPTH) ** 0.25
LN_EPS = 1e-5
RMS_EPS = 1e-6

LANES = 128
SUBLANES = 8
NEG_INF = float("-inf")
M_INIT = -1e30
VMEM_LIMIT = 48 * 1024 * 1024


def _params(*sem):
    return pltpu.CompilerParams(dimension_semantics=sem, vmem_limit_bytes=VMEM_LIMIT)


def _ln(z, g, b):
    mu = jnp.mean(z, axis=-1, keepdims=True)
    zc = z - mu
    var = jnp.mean(zc * zc, axis=-1, keepdims=True)
    return zc * lax.rsqrt(var + LN_EPS) * g + b


def _div_pow2(x, d):
    assert d & (d - 1) == 0
    return x >> (d.bit_length() - 1)


def _is_pow2(v):
    m, _ = np.frexp(v)
    return m == 0.5


def _proj_kernel(*refs, n_out, rope, shift, rms, chunk):
    x_ref = refs[0]
    pos = 1
    if rms:
        g_ref = refs[pos]
        pos += 1
    w_refs = refs[pos:pos + n_out]
    pos += n_out
    if any(rope):
        c_ref, s1_ref, s2_ref = refs[pos:pos + 3]
        pos += 3
    o_refs = refs[pos:pos + n_out]

    x = x_ref[...]
    if rms:
        xf = x.astype(F32)
        xf = xf * lax.rsqrt(jnp.mean(xf * xf, axis=-1, keepdims=True) + RMS_EPS) * g_ref[...]
        xb = xf.astype(BF16)
    else:
        xb = x.astype(BF16)
    for i in range(n_out):
        n = w_refs[i].shape[1]
        for c0 in range(0, n, chunk):
            cw = min(chunk, n - c0)
            acc = jnp.dot(xb, w_refs[i][:, c0:c0 + cw], preferred_element_type=F32)
            if rope[i]:
                reps = cw // LANES
                c, s1, s2 = c_ref[...], s1_ref[...], s2_ref[...]
                if reps > 1:
                    c = jnp.concatenate([c] * reps, axis=1)
                    s1 = jnp.concatenate([s1] * reps, axis=1)
                    s2 = jnp.concatenate([s2] * reps, axis=1)
                acc = acc * c + pltpu.roll(acc, shift, 1) * s1 + pltpu.roll(acc, cw - shift, 1) * s2
            o_refs[i][:, c0:c0 + cw] = acc.astype(o_refs[i].dtype)


def _proj(x, ws, out_dtypes, *, rope=None, tables=None, shift=0, rms_g=None, tm=512):
    m, k = x.shape
    n_out = len(ws)
    rope = tuple(rope) if rope is not None else (False,) * n_out
    tm = min(tm, m)
    assert m % tm == 0
    in_specs = [pl.BlockSpec((tm, k), lambda i: (i, 0))]
    args = [x]
    if rms_g is not None:
        in_specs.append(pl.BlockSpec((1, k), lambda i: (0, 0)))
        args.append(rms_g.reshape(1, k).astype(F32))
    for w in ws:
        assert w.shape[1] % LANES == 0
        in_specs.append(pl.BlockSpec(w.shape, lambda i: (0, 0)))
        args.append(w)
    if any(rope):
        for t in tables:
            in_specs.append(pl.BlockSpec((tm, LANES), lambda i: (i, 0)))
            args.append(t)
    out_shape = [jax.ShapeDtypeStruct((m, w.shape[1]), dt) for w, dt in zip(ws, out_dtypes)]
    out_specs = [pl.BlockSpec((tm, w.shape[1]), lambda i: (i, 0)) for w in ws]
    kern = functools.partial(_proj_kernel, n_out=n_out, rope=rope, shift=shift,
                             rms=rms_g is not None, chunk=2 * LANES)
    return pl.pallas_call(kern, out_shape=out_shape, grid=(m // tm,), in_specs=in_specs,
                          out_specs=out_specs, compiler_params=_params("parallel"), name="proj")(*args)


def _rope_tables(positions, rot_dim, period):
    half = rot_dim // 2
    inv = ROPE_THETA ** (-jnp.arange(0, rot_dim, 2, dtype=F32) / rot_dim)
    ang = positions.astype(F32).reshape(-1)[:, None] * inv
    cos, sin = jnp.cos(ang), jnp.sin(ang)
    t = ang.shape[0]
    zero_h = jnp.zeros((t, half), F32)
    rest = period - rot_dim
    c = jnp.concatenate([cos, cos, jnp.ones((t, rest), F32)], axis=1)
    s1 = jnp.concatenate([zero_h, sin, jnp.zeros((t, rest), F32)], axis=1)
    s2 = jnp.concatenate([-sin, zero_h, jnp.zeros((t, rest), F32)], axis=1)
    reps = LANES // period
    return tuple(jnp.tile(a, (1, reps)) for a in (c, s1, s2))


def _mm_res_ln_kernel(a_ref, w_ref, h_ref, g_ref, b_ref, o_ref):
    y = jnp.dot(a_ref[...].astype(BF16), w_ref[...], preferred_element_type=F32)
    o_ref[...] = _ln(DEEPNORM_ALPHA * h_ref[...] + y, g_ref[...], b_ref[...])


def _mm_res_ln(a, w, h, g, b, tm=512):
    m, k = a.shape
    d = w.shape[1]
    tm = min(tm, m)
    return pl.pallas_call(
        _mm_res_ln_kernel,
        out_shape=jax.ShapeDtypeStruct((m, d), F32),
        grid=(m // tm,),
        in_specs=[pl.BlockSpec((tm, k), lambda i: (i, 0)),
                  pl.BlockSpec((k, d), lambda i: (0, 0)),
                  pl.BlockSpec((tm, d), lambda i: (i, 0)),
                  pl.BlockSpec((1, d), lambda i: (0, 0)),
                  pl.BlockSpec((1, d), lambda i: (0, 0))],
        out_specs=pl.BlockSpec((tm, d), lambda i: (i, 0)),
        compiler_params=_params("parallel"), name="mm_res_ln",
    )(a, w, h, g.reshape(1, d), b.reshape(1, d))


def _add_ln_kernel(h_ref, a_ref, b2_ref, g_ref, b_ref, o_ref):
    z = DEEPNORM_ALPHA * h_ref[...] + (a_ref[...].astype(F32) + b2_ref[...].astype(F32))
    o_ref[...] = _ln(z, g_ref[...], b_ref[...])


def _add_ln(h, a, b2, g, b, tm=512):
    m, d = h.shape
    tm = min(tm, m)
    row = pl.BlockSpec((tm, d), lambda i: (i, 0))
    vec = pl.BlockSpec((1, d), lambda i: (0, 0))
    return pl.pallas_call(
        _add_ln_kernel, out_shape=jax.ShapeDtypeStruct((m, d), F32), grid=(m // tm,),
        in_specs=[row, row, row, vec, vec], out_specs=row,
        compiler_params=_params("parallel"), name="add_ln",
    )(h, a, b2, g.reshape(1, d), b.reshape(1, d))


def _flash_t(qts, k_refs, vt_refs, q0, *, tq, tk, scale, window=None, m0s=None, bias_refs=None, blk=None):
    n = len(qts)
    dv = vt_refs[0].shape[1]
    fold = _is_pow2(scale)
    if fold:
        qts = [q * jnp.asarray(scale, q.dtype) for q in qts]
    qpos = q0 + lax.broadcasted_iota(jnp.int32, (1, tq), 1)
    carry = []
    for c in range(n):
        if m0s is None:
            carry += [jnp.full((1, tq), M_INIT, F32), jnp.zeros((1, tq), F32)]
        else:
            carry += [m0s[c], jnp.ones((1, tq), F32)]
        carry.append(jnp.zeros((dv, tq), F32))

    def make_body(masked):
        def body(j, carry):
            start = pl.multiple_of(j * tk, tk)
            if masked:
                delta = qpos - (start + lax.broadcasted_iota(jnp.int32, (tk, 1), 0))
                mask = delta >= 0
                if window is not None:
                    mask = mask & (delta < window)
            k_blocks, v_blocks, biases = {}, {}, {}
            out = []
            for c in range(n):
                m, l, acc = carry[3 * c:3 * c + 3]
                if id(k_refs[c]) not in k_blocks:
                    k_blocks[id(k_refs[c])] = k_refs[c][pl.ds(start, tk), :]
                    v_blocks[id(k_refs[c])] = vt_refs[c][j]
                s = jnp.dot(k_blocks[id(k_refs[c])], qts[c], preferred_element_type=F32)
                if not fold:
                    s = s * scale
                if bias_refs is not None:
                    br = bias_refs[c]
                    if id(br) not in biases:
                        per = tk // blk
                        bias = br[j, per - 1:per, :]
                        if per > 1:
                            sub = lax.broadcasted_iota(jnp.int32, (tk, tq), 0)
                            bias = jnp.broadcast_to(bias, (tk, tq))
                            for r in range(per - 2, -1, -1):
                                bias = jnp.where(sub < (r + 1) * blk, br[j, r:r + 1, :], bias)
                        biases[id(br)] = bias
                    s = s + biases[id(br)]
                if masked:
                    s = jnp.where(mask, s, NEG_INF)
                m_new = jnp.maximum(m, jnp.max(s, axis=0, keepdims=True))
                a = jnp.exp(m - m_new)
                p = jnp.exp(s - m_new)
                l_new = a * l + jnp.sum(p, axis=0, keepdims=True)
                acc_new = a * acc + jnp.dot(v_blocks[id(k_refs[c])], p.astype(BF16), preferred_element_type=F32)
                out += [m_new, l_new, acc_new]
            return tuple(out)
        return body

    diag = _div_pow2(q0, tk)
    hi = _div_pow2(q0 + tq - 1, tk)
    carry = tuple(carry)
    if window is None:
        lo_full = 0
    else:
        lo = _div_pow2(jnp.maximum(q0 - (window - 1), 0), tk)
        lo_full = jnp.minimum(_div_pow2(jnp.maximum(q0 + tq - window + tk - 1, 0), tk), diag)
        carry = lax.fori_loop(lo, lo_full, make_body(True), carry)
    carry = lax.fori_loop(lo_full, diag, make_body(False), carry)
    carry = lax.fori_loop(diag, hi + 1, make_body(True), carry)
    return [carry[3 * c + 2] / jnp.maximum(carry[3 * c + 1], 1e-30) for c in range(n)]


def _topk_rows(score, valid, k):
    row = lax.broadcasted_iota(jnp.int32, score.shape, 0)
    rank = jnp.zeros(score.shape, F32)
    for j in range(score.shape[0]):
        r = score[j:j + 1, :]
        beats = (r > score) | ((r == score) & (row > j))
        rank = rank + jnp.where(beats, 1.0, 0.0)
    return jnp.where(valid & (rank < k), 1.0, 0.0)


def _fill_bias(bias_ref, sel, per):
    bias = jnp.where(sel > 0.5, 0.0, NEG_INF)
    for jb in range(bias_ref.shape[0]):
        bias_ref[jb, 0:per, :] = bias[jb * per:(jb + 1) * per, :]


def _attn_kernel(*refs, hb, rep, tq, tk, scale, window, has_sink):
    if has_sink:
        qt_ref, k_ref, vt_ref, sink_ref, o_ref = refs
    else:
        qt_ref, k_ref, vt_ref, o_ref = refs
    q0 = pl.program_id(2) * tq
    heads = [(hh, r) for hh in range(hb) for r in range(rep)]
    k_views = [k_ref.at[hh] for hh in range(hb)]
    vt_views = [vt_ref.at[hh] for hh in range(hb)]
    outs = _flash_t([qt_ref[hh, r] for hh, r in heads], [k_views[hh] for hh, _ in heads],
                    [vt_views[hh] for hh, _ in heads], q0, tq=tq, tk=tk, scale=scale, window=window,
                    m0s=[sink_ref[hh, r] for hh, r in heads] if has_sink else None)
    for (hh, r), o in zip(heads, outs):
        o_ref[hh, r] = o.astype(o_ref.dtype)


def _attention(qt, k, vt, *, scale, window=None, sinks=None, tq=128, hb=1):
    b, g, rep, dk, s = qt.shape
    nblk, dv, tk = vt.shape[2:]
    tq = min(tq, s)
    assert g % hb == 0
    in_specs = [pl.BlockSpec((None, hb, rep, dk, tq), lambda bi, gi, i: (bi, gi, 0, 0, i)),
                pl.BlockSpec((None, hb, s, dk), lambda bi, gi, i: (bi, gi, 0, 0)),
                pl.BlockSpec((None, hb, nblk, dv, tk), lambda bi, gi, i: (bi, gi, 0, 0, 0))]
    args = [qt, k, vt]
    if sinks is not None:
        in_specs.append(pl.BlockSpec((hb, rep, 1, tq), lambda bi, gi, i: (gi, 0, 0, 0)))
        args.append(sinks)
    kern = functools.partial(_attn_kernel, hb=hb, rep=rep, tq=tq, tk=tk, scale=scale, window=window,
                             has_sink=sinks is not None)
    return pl.pallas_call(
        kern, out_shape=jax.ShapeDtypeStruct((b, g, rep, dv, s), BF16), grid=(b, g // hb, s // tq),
        in_specs=in_specs,
        out_specs=pl.BlockSpec((None, hb, rep, dv, tq), lambda bi, gi, i: (bi, gi, 0, 0, i)),
        compiler_params=_params("parallel", "parallel", "parallel"), name="attention",
    )(*args)


def _heads_first(t, b, s, h):
    return t.reshape(b, s, h, t.shape[-1] // h).transpose(0, 2, 1, 3)


def _heads_t(t, b, s, h):
    return t.reshape(b, s, h, t.shape[-1] // h).transpose(0, 2, 3, 1)


def _vt_blocks(t, b, s, h, tk):
    return t.reshape(b, s // tk, tk, h, t.shape[-1] // h).transpose(0, 3, 1, 4, 2)


def _tokens_major(ot, b, s):
    return ot.reshape(b, -1, s).transpose(0, 2, 1).reshape(b * s, -1)


def _gelu_tanh(x):
    return 0.5 * x * (1.0 + jnp.tanh(np.sqrt(2.0 / np.pi).astype(np.float32) * (x + 0.044715 * (x * x * x))))


def _nsa_compress_kernel(ak_ref, av_ref, pe_ref, w1_ref, w2_ref, kc_ref, vc_ref):
    half = ak_ref.shape[-1]
    ncp = ak_ref.shape[0]
    for j, (a_ref, o_ref) in enumerate(((ak_ref, kc_ref), (av_ref, vc_ref))):
        a = a_ref[...].astype(F32)
        lo = (a + pe_ref[j, 0:1, :]).astype(BF16)
        hi = (a + pe_ref[j, 1:2, :]).astype(BF16)
        h1 = jnp.dot(lo, w1_ref[j, :half, :], preferred_element_type=F32)
        h2 = jnp.dot(hi, w1_ref[j, half:, :], preferred_element_type=F32)
        h = _gelu_tanh(h1 + pltpu.roll(h2, ncp - 1, 0))
        o_ref[...] = jnp.dot(h.astype(BF16), w2_ref[j], preferred_element_type=F32).astype(o_ref.dtype)


def _nsa_compress(ak, av, pe2, w1, w2):
    b, g, ncp, wid = ak.shape
    dh = w2.shape[-1]
    blk = pl.BlockSpec((None, None, ncp, wid), lambda bi, gi: (bi, gi, 0, 0))
    full = lambda a: pl.BlockSpec(a.shape, lambda bi, gi: (0,) * a.ndim)
    out = pl.BlockSpec((None, None, ncp, dh), lambda bi, gi: (bi, gi, 0, 0))
    return pl.pallas_call(
        _nsa_compress_kernel,
        out_shape=[jax.ShapeDtypeStruct((b, g, ncp, dh), BF16)] * 2,
        grid=(b, g), in_specs=[blk, blk, full(pe2), full(w1), full(w2)], out_specs=[out, out],
        compiler_params=_params("parallel", "parallel"), name="nsa_compress",
    )(ak, av, pe2, w1, w2)


def _nsa_attn_kernel(qt_ref, kc_ref, vct_ref, ks_ref, vst_ref, kw_ref, vwt_ref, gl_ref, ovlt_ref, o_ref,
                     bias_ref, *, rep, tq, tk, nc, n_sel, scale):
    q0 = pl.program_id(2) * tq
    rows = rep * tq
    qts = [qt_ref[r] for r in range(rep)]
    qt = jnp.concatenate(qts, axis=1)
    row_pos = q0 + (lax.broadcasted_iota(jnp.int32, (1, rows), 1) & (tq - 1))

    ncp = kc_ref.shape[0]
    s = jnp.dot(kc_ref[...], qt * jnp.asarray(scale, qt.dtype), preferred_element_type=F32)
    cidx = lax.broadcasted_iota(jnp.int32, (ncp, 1), 0)
    cmask = (cidx * NSA_CMP_STRIDE + (NSA_CMP_BLOCK - 1) <= row_pos) & (cidx < nc)
    s = jnp.where(cmask, s, NEG_INF)
    m = jnp.max(s, axis=0, keepdims=True)
    m = jnp.where(m == NEG_INF, 0.0, m)
    e = jnp.exp(s - m)
    p = e / jnp.maximum(jnp.sum(e, axis=0, keepdims=True), 1e-30)
    o_cmp = jnp.dot(vct_ref[...], p.astype(BF16), preferred_element_type=F32)

    psum = p[:, 0:tq]
    for r in range(1, rep):
        psum = psum + p[:, r * tq:(r + 1) * tq]
    p_hi = psum.astype(BF16)
    p_lo = (psum - p_hi.astype(F32)).astype(BF16)
    ovlt = ovlt_ref[...]
    imp = (jnp.dot(ovlt, p_hi, preferred_element_type=F32)
           + jnp.dot(ovlt, p_lo, preferred_element_type=F32))
    nb = imp.shape[0]
    blk = lax.broadcasted_iota(jnp.int32, (nb, tq), 0)
    cur = _div_pow2(q0 + lax.broadcasted_iota(jnp.int32, (nb, tq), 1), NSA_SLC_BLOCK)
    forced = (blk == 0) | (blk == cur) | (blk == cur - 1)
    valid = blk <= cur
    score = jnp.where(valid, jnp.where(forced, NSA_FORCE_SCORE, imp), NEG_INF)
    _fill_bias(bias_ref, _topk_rows(score, valid, n_sel), tk // NSA_SLC_BLOCK)

    o_slc = _flash_t(qts, [ks_ref] * rep, [vst_ref] * rep, q0, tq=tq, tk=tk, scale=scale,
                     bias_refs=[bias_ref] * rep, blk=NSA_SLC_BLOCK)
    o_win = _flash_t(qts, [kw_ref] * rep, [vwt_ref] * rep, q0, tq=tq, tk=tk, scale=scale, window=NSA_WINDOW)

    for r in range(rep):
        gate = jax.nn.sigmoid(gl_ref[r])
        o = (gate[0:1, :] * o_cmp[:, r * tq:(r + 1) * tq] + gate[1:2, :] * o_slc[r] + gate[2:3, :] * o_win[r])
        o_ref[r] = o.astype(o_ref.dtype)


def _nsa_attention(qt, kc, vct, ks, vst, kw, vwt, gl, ovlt, *, nc, tq=128):
    b, g, rep, dh, s = qt.shape
    ncp = kc.shape[2]
    nblk, _, tk = vst.shape[2:]
    nb = s // NSA_SLC_BLOCK
    tq = min(tq, s)
    qspec = pl.BlockSpec((None, None, rep, dh, tq), lambda bi, gi, i: (bi, gi, 0, 0, i))
    kspec = pl.BlockSpec((None, None, s, dh), lambda bi, gi, i: (bi, gi, 0, 0))
    vspec = pl.BlockSpec((None, None, nblk, dh, tk), lambda bi, gi, i: (bi, gi, 0, 0, 0))
    kern = functools.partial(_nsa_attn_kernel, rep=rep, tq=tq, tk=tk, nc=nc,
                             n_sel=min(NSA_SLC_TOPN, nb), scale=dh ** -0.5)
    return pl.pallas_call(
        kern, out_shape=jax.ShapeDtypeStruct((b, g, rep, dh, s), BF16), grid=(b, g, s // tq),
        in_specs=[qspec,
                  pl.BlockSpec((None, None, ncp, dh), lambda bi, gi, i: (bi, gi, 0, 0)),
                  pl.BlockSpec((None, None, dh, ncp), lambda bi, gi, i: (bi, gi, 0, 0)),
                  kspec, vspec, kspec, vspec,
                  pl.BlockSpec((None, None, rep, gl.shape[3], tq), lambda bi, gi, i: (bi, gi, 0, 0, i)),
                  pl.BlockSpec(ovlt.shape, lambda bi, gi, i: (0, 0))],
        out_specs=qspec,
        scratch_shapes=[pltpu.VMEM((nblk, SUBLANES, tq), F32)],
        compiler_params=_params("parallel", "parallel", "parallel"), name="nsa_attention",
    )(qt, kc, vct, ks, vst, kw, vwt, gl, ovlt)


def _nsa_mixer(h, b, s, tab_p, w_in, cmp_pe, cmp_w1, cmp_w2):
    hn, g, dh = N_HEADS, NSA_KV_GROUPS, HEAD_DIM
    rep = hn // g
    gd = g * dh
    q_end = hn * dh
    tk = min(128, s)
    kv = lambda j: w_in[:, q_end + j * gd:q_end + (j + 1) * gd]
    w_rope = jnp.concatenate([w_in[:, :q_end], kv(0), kv(2), kv(4)], axis=1).astype(BF16)
    w_plain = jnp.concatenate([kv(1), kv(3), kv(5)], axis=1).astype(BF16)
    n_gate = 3 * hn
    w_gate = jnp.pad(w_in[:, q_end + 6 * gd:], ((0, 0), (0, LANES - n_gate))).astype(BF16)
    roped, plain, gates = _proj(h, [w_rope, w_plain, w_gate], [BF16, BF16, F32],
                                rope=[True, False, False], tables=tab_p, shift=ROPE_DIM // 2)
    qt = _heads_t(roped[:, :q_end], b, s, hn).reshape(b, g, rep, dh, s)
    k_c, k_s, k_w = [_heads_first(roped[:, q_end + j * gd:q_end + (j + 1) * gd], b, s, g) for j in range(3)]
    v_c = _heads_first(plain[:, :gd], b, s, g)
    vst, vwt = [_vt_blocks(plain[:, j * gd:(j + 1) * gd], b, s, g, tk) for j in (1, 2)]
    gl = gates[:, :n_gate].reshape(b, s, 3, g, rep).transpose(0, 3, 4, 2, 1)

    nc = (s - NSA_CMP_BLOCK) // NSA_CMP_STRIDE + 1
    ncp = s // NSA_CMP_STRIDE
    half = NSA_CMP_STRIDE * dh
    ak = k_c.reshape(b, g, ncp, half)
    av = v_c.reshape(b, g, ncp, half)
    pe2 = cmp_pe.reshape(2, 2, half).astype(F32)
    kc, vc = _nsa_compress(ak, av, pe2, cmp_w1.astype(BF16), cmp_w2.astype(BF16))
    vct = vc.transpose(0, 1, 3, 2)

    nb = s // NSA_SLC_BLOCK
    c_start = np.arange(ncp) * NSA_CMP_STRIDE
    b_start = np.arange(nb) * NSA_SLC_BLOCK
    overlap = np.clip(np.minimum(c_start[:, None] + NSA_CMP_BLOCK, b_start[None, :] + NSA_SLC_BLOCK)
                      - np.maximum(c_start[:, None], b_start[None, :]), 0, None) / NSA_CMP_BLOCK
    overlap[nc:] = 0.0
    ovlt = jnp.asarray(overlap.T, BF16)
    ot = _nsa_attention(qt, kc, vct, k_s, vst, k_w, vwt, gl, ovlt, nc=nc)
    return _tokens_major(ot, b, s)


def _mla_mixer(h, b, s, tab_m, w_down, q_norm, kv_norm, w_uq, w_ukv):
    hn = N_HEADS
    qr_, kvr = MLA_Q_RANK, MLA_KV_RANK
    w_kr = jnp.pad(w_down[:, qr_ + kvr:], ((0, 0), (0, LANES - MLA_ROPE_DIM)))
    cq, ckv, kr = _proj(h, [w_down[:, :qr_].astype(BF16), w_down[:, qr_:qr_ + kvr].astype(BF16),
                            w_kr.astype(BF16)], [F32, F32, BF16],
                        rope=[False, False, True], tables=tab_m, shift=MLA_ROPE_DIM // 2)
    w_uq3 = w_uq.reshape(qr_, hn, MLA_NOPE_DIM + MLA_ROPE_DIM)
    w_qn = w_uq3[:, :, :MLA_NOPE_DIM].reshape(qr_, hn * MLA_NOPE_DIM).astype(BF16)
    w_qr = w_uq3[:, :, MLA_NOPE_DIM:].reshape(qr_, hn * MLA_ROPE_DIM).astype(BF16)
    qn, qr = _proj(cq, [w_qn, w_qr], [BF16, BF16], rope=[False, True], tables=tab_m,
                   shift=MLA_ROPE_DIM // 2, rms_g=q_norm)
    w_ukv3 = w_ukv.reshape(kvr, hn, MLA_NOPE_DIM + MLA_V_DIM)
    w_kn = w_ukv3[:, :, :MLA_NOPE_DIM].reshape(kvr, hn * MLA_NOPE_DIM).astype(BF16)
    w_v = w_ukv3[:, :, MLA_NOPE_DIM:].reshape(kvr, hn * MLA_V_DIM).astype(BF16)
    kn, v = _proj(ckv, [w_kn, w_v], [BF16, BF16], rms_g=kv_norm)

    tk = min(256, s)
    dpad = LANES - MLA_NOPE_DIM - MLA_ROPE_DIM
    qt = jnp.concatenate([_heads_t(qn, b, s, hn), _heads_t(qr, b, s, hn),
                          jnp.zeros((b, hn, dpad, s), BF16)], axis=2)
    k_rope = jnp.broadcast_to(kr[:, :MLA_ROPE_DIM].reshape(b, 1, s, MLA_ROPE_DIM), (b, hn, s, MLA_ROPE_DIM))
    k = jnp.concatenate([_heads_first(kn, b, s, hn), k_rope, jnp.zeros((b, hn, s, dpad), BF16)], axis=-1)
    scale = (MLA_NOPE_DIM + MLA_ROPE_DIM) ** -0.5
    ot = _attention(qt[:, :, None], k, _vt_blocks(v, b, s, hn, tk), scale=scale, tq=256, hb=4)
    return _tokens_major(ot, b, s)


def _moba_kernel(qt_ref, k_ref, vt_ref, o_ref, bias_ref, *, hb, tq, n_sel, scale):
    q0 = pl.program_id(2) * tq
    s_len, dh = k_ref.shape[1:]
    nb = s_len // MOBA_BLOCK
    nbp = 2 * SUBLANES
    sub = lax.broadcasted_iota(jnp.int32, (nbp, dh), 0)
    blk = lax.broadcasted_iota(jnp.int32, (nb, tq), 0)
    own = _div_pow2(q0 + lax.broadcasted_iota(jnp.int32, (nb, tq), 1), MOBA_BLOCK)
    valid = blk < own
    qts = [qt_ref[hh] for hh in range(hb)]
    for hh in range(hb):
        k_mean = jnp.zeros((nbp, dh), F32)
        for j in range(nb):
            kj = k_ref[hh, j * MOBA_BLOCK:(j + 1) * MOBA_BLOCK, :].astype(F32)
            k_mean = jnp.where(sub == j, jnp.sum(kj, axis=0, keepdims=True) * (1.0 / MOBA_BLOCK), k_mean)
        km_hi = k_mean.astype(BF16)
        km_lo = (k_mean - km_hi.astype(F32)).astype(BF16)
        gate = (jnp.dot(km_hi, qts[hh], preferred_element_type=F32)
                + jnp.dot(km_lo, qts[hh], preferred_element_type=F32))[:nb]
        score = jnp.where(valid, gate, NEG_INF)
        sel = jnp.maximum(_topk_rows(score, valid, n_sel), jnp.where(blk == own, 1.0, 0.0))
        _fill_bias(bias_ref.at[hh], sel, 1)
    outs = _flash_t(qts, [k_ref.at[hh] for hh in range(hb)], [vt_ref.at[hh] for hh in range(hb)], q0,
                    tq=tq, tk=MOBA_BLOCK, scale=scale,
                    bias_refs=[bias_ref.at[hh] for hh in range(hb)], blk=MOBA_BLOCK)
    for hh in range(hb):
        o_ref[hh] = outs[hh].astype(o_ref.dtype)


def _moba_attention(qt, k, vt, tq=256, hb=4):
    b, hn, dh, s = qt.shape
    nb = s // MOBA_BLOCK
    n_sel = max(1, min(MOBA_TOPK, nb - 1))
    tq = min(tq, s)
    assert nb <= 2 * SUBLANES and hn % hb == 0
    qspec = pl.BlockSpec((None, hb, dh, tq), lambda bi, hi, i: (bi, hi, 0, i))
    kern = functools.partial(_moba_kernel, hb=hb, tq=tq, n_sel=n_sel, scale=dh ** -0.5)
    return pl.pallas_call(
        kern, out_shape=jax.ShapeDtypeStruct((b, hn, dh, s), BF16), grid=(b, hn // hb, s // tq),
        in_specs=[qspec,
                  pl.BlockSpec((None, hb, s, dh), lambda bi, hi, i: (bi, hi, 0, 0)),
                  pl.BlockSpec((None, hb, nb, dh, MOBA_BLOCK), lambda bi, hi, i: (bi, hi, 0, 0, 0))],
        out_specs=qspec,
        scratch_shapes=[pltpu.VMEM((hb, nb, SUBLANES, tq), F32)],
        compiler_params=_params("parallel", "parallel", "parallel"), name="moba_attention",
    )(qt, k, vt)


def _moba_mixer(h, b, s, tab_p, w_in):
    hn, dh = N_HEADS, HEAD_DIM
    assert s % MOBA_BLOCK == 0
    qk, v = _proj(h, [w_in[:, :2 * hn * dh].astype(BF16), w_in[:, 2 * hn * dh:].astype(BF16)], [BF16, BF16],
                  rope=[True, False], tables=tab_p, shift=ROPE_DIM // 2)
    qt = _heads_t(qk[:, :hn * dh], b, s, hn)
    k = _heads_first(qk[:, hn * dh:], b, s, hn)
    ot = _moba_attention(qt, k, _vt_blocks(v, b, s, hn, MOBA_BLOCK))
    return _tokens_major(ot, b, s)


def _swa_mixer(h, b, s, tab_p, w_in, sinks):
    hn, g, dh = N_HEADS, SWA_KV_HEADS, HEAD_DIM
    rep = hn // g
    tq = min(128, s)
    qk, v = _proj(h, [w_in[:, :(hn + g) * dh].astype(BF16), w_in[:, (hn + g) * dh:].astype(BF16)], [BF16, BF16],
                  rope=[True, False], tables=tab_p, shift=ROPE_DIM // 2)
    qt = _heads_t(qk[:, :hn * dh], b, s, hn).reshape(b, g, rep, dh, s)
    k = _heads_first(qk[:, hn * dh:], b, s, g)
    sink_rows = jnp.broadcast_to(sinks.astype(F32).reshape(g, rep, 1, 1), (g, rep, 1, tq))
    ot = _attention(qt, k, _vt_blocks(v, b, s, g, tq), scale=dh ** -0.5, window=SWA_WINDOW,
                    sinks=sink_rows, tq=tq)
    return _tokens_major(ot, b, s)


def _xattn_kernel(q_ref, kv_ref, o_ref, *, scale):
    d = q_ref.shape[-1]
    hd = XATTN_HEAD_DIM
    for hh in range(XATTN_HEADS):
        q = q_ref[:, hh * hd:(hh + 1) * hd]
        k = kv_ref[:, hh * hd:(hh + 1) * hd]
        v = kv_ref[:, d + hh * hd:d + (hh + 1) * hd]
        s = lax.dot_general(q, k, (((1,), (1,)), ((), ())), preferred_element_type=F32) * scale
        m = jnp.max(s, axis=-1, keepdims=True)
        e = jnp.exp(s - m)
        p = e / jnp.sum(e, axis=-1, keepdims=True)
        o_ref[:, hh * hd:(hh + 1) * hd] = jnp.dot(p.astype(BF16), v, preferred_element_type=F32).astype(o_ref.dtype)


def _cross_attention(q, kv, b, s, tq=512):
    d = q.shape[-1]
    mlen = kv.shape[0] // b
    tq = min(tq, s)
    q3 = q.reshape(b, s, d)
    kv3 = kv.reshape(b, mlen, 2 * d)
    o = pl.pallas_call(
        functools.partial(_xattn_kernel, scale=XATTN_HEAD_DIM ** -0.5),
        out_shape=jax.ShapeDtypeStruct((b, s, d), BF16), grid=(b, s // tq),
        in_specs=[pl.BlockSpec((None, tq, d), lambda bi, i: (bi, i, 0)),
                  pl.BlockSpec((None, mlen, 2 * d), lambda bi, i: (bi, 0, 0))],
        out_specs=pl.BlockSpec((None, tq, d), lambda bi, i: (bi, i, 0)),
        compiler_params=_params("parallel", "parallel"), name="cross_attention",
    )(q3, kv3)
    return o.reshape(b * s, d)


def _swiglu_partial(xb, wg_ref, wu_ref, wo_ref):
    gte = jnp.dot(xb, wg_ref[...], preferred_element_type=F32)
    up = jnp.dot(xb, wu_ref[...], preferred_element_type=F32)
    act = gte * jax.nn.sigmoid(gte) * up
    return jnp.dot(act.astype(BF16), wo_ref[...], preferred_element_type=F32)


def _ffn_ln_kernel(x_ref, wg_ref, wu_ref, wo_ref, g_ref, b_ref, o_ref, acc_ref):
    f = pl.program_id(1)

    @pl.when(f == 0)
    def _():
        acc_ref[...] = jnp.zeros_like(acc_ref)

    acc_ref[...] += _swiglu_partial(x_ref[...].astype(BF16), wg_ref, wu_ref, wo_ref)

    @pl.when(f == pl.num_programs(1) - 1)
    def _():
        o_ref[...] = _ln(DEEPNORM_ALPHA * x_ref[...] + acc_ref[...], g_ref[...], b_ref[...])


def _ffn_ln(h, w_in, w_out, g, b, tm=1024, tf=512):
    m, d = h.shape
    dff = w_out.shape[0]
    tm = min(tm, m)
    nf = dff // tf
    row = pl.BlockSpec((tm, d), lambda i, f: (i, 0))
    vec = pl.BlockSpec((1, d), lambda i, f: (0, 0))
    return pl.pallas_call(
        _ffn_ln_kernel, out_shape=jax.ShapeDtypeStruct((m, d), F32), grid=(m // tm, nf),
        in_specs=[row,
                  pl.BlockSpec((d, tf), lambda i, f: (0, f)),
                  pl.BlockSpec((d, tf), lambda i, f: (0, nf + f)),
                  pl.BlockSpec((tf, d), lambda i, f: (f, 0)),
                  vec, vec],
        out_specs=row, scratch_shapes=[pltpu.VMEM((tm, d), F32)],
        compiler_params=_params("parallel", "arbitrary"), name="ffn_ln",
    )(h, w_in.astype(BF16), w_in.astype(BF16), w_out.astype(BF16), g.reshape(1, d), b.reshape(1, d))


def _router_kernel(x_ref, rh_ref, rl_ref, b_ref, comb_ref, sel_ref):
    x = x_ref[...]
    xh = x.astype(BF16)
    xl = (x - xh.astype(F32)).astype(BF16)
    rh, rl = rh_ref[...], rl_ref[...]
    lg = (jnp.dot(xh, rh, preferred_element_type=F32) + jnp.dot(xl, rh, preferred_element_type=F32)
          + jnp.dot(xh, rl, preferred_element_type=F32)) + b_ref[...]
    lane = lax.broadcasted_iota(jnp.int32, lg.shape, 1).astype(F32)
    lg = jnp.where(lane < N_EXPERTS, lg, NEG_INF)
    m1 = jnp.max(lg, axis=-1, keepdims=True)
    i1 = jnp.min(jnp.where(lg == m1, lane, float(LANES)), axis=-1, keepdims=True)
    lg2 = jnp.where(lane == i1, NEG_INF, lg)
    m2 = jnp.max(lg2, axis=-1, keepdims=True)
    i2 = jnp.min(jnp.where(lg2 == m2, lane, float(LANES)), axis=-1, keepdims=True)
    e2 = jnp.exp(m2 - m1)
    den = 1.0 + e2
    comb_ref[...] = jnp.where(lane == i1, 1.0 / den, 0.0) + jnp.where(lane == i2, e2 / den, 0.0)
    sel_ref[...] = jnp.where((lane == i1) | (lane == i2), 1.0, 0.0)


def _router(h, router, bias, tm=512):
    m, d = h.shape
    tm = min(tm, m)
    rp = jnp.pad(router.astype(F32), ((0, 0), (0, LANES - N_EXPERTS)))
    rh = rp.astype(BF16)
    rl = (rp - rh.astype(F32)).astype(BF16)
    bp = jnp.pad(bias.astype(F32), (0, LANES - N_EXPERTS)).reshape(1, LANES)
    row = pl.BlockSpec((tm, d), lambda i: (i, 0))
    wsp = pl.BlockSpec((d, LANES), lambda i: (0, 0))
    osp = pl.BlockSpec((tm, LANES), lambda i: (i, 0))
    return pl.pallas_call(
        _router_kernel, out_shape=[jax.ShapeDtypeStruct((m, LANES), F32)] * 2, grid=(m // tm,),
        in_specs=[row, wsp, wsp, pl.BlockSpec((1, LANES), lambda i: (0, 0))], out_specs=[osp, osp],
        compiler_params=_params("parallel"), name="moe_router",
    )(h, rh, rl, bp)


def _moe_ffn_kernel(te_ref, tv_ref, x_ref, wg_ref, wu_ref, wo_ref, rs_ref, o_ref, acc_ref):
    i = pl.program_id(0)
    f = pl.program_id(1)

    @pl.when(f == 0)
    def _():
        acc_ref[...] = jnp.zeros_like(acc_ref)

    @pl.when(tv_ref[i] > 0)
    def _():
        acc_ref[...] += _swiglu_partial(x_ref[...], wg_ref, wu_ref, wo_ref)

    @pl.when(f == pl.num_programs(1) - 1)
    def _():
        o_ref[...] = acc_ref[...] * rs_ref[...]


def _moe_ffn(xs, w_in, w_out, tile_expert, tile_valid, rowscale, tm, tf=512):
    tp, d = xs.shape
    dff = w_out.shape[1]
    nf = dff // tf
    last = nf - 1

    def fblk(f, tv, i):
        return jnp.where(tv[i] > 0, f, last)

    grid_spec = pltpu.PrefetchScalarGridSpec(
        num_scalar_prefetch=2, grid=(tp // tm, nf),
        in_specs=[pl.BlockSpec((tm, d), lambda i, f, te, tv: (i, 0)),
                  pl.BlockSpec((None, d, tf), lambda i, f, te, tv: (te[i], 0, fblk(f, tv, i))),
                  pl.BlockSpec((None, d, tf), lambda i, f, te, tv: (te[i], 0, nf + fblk(f, tv, i))),
                  pl.BlockSpec((None, tf, d), lambda i, f, te, tv: (te[i], fblk(f, tv, i), 0)),
                  pl.BlockSpec((tm, 1), lambda i, f, te, tv: (i, 0))],
        out_specs=pl.BlockSpec((tm, d), lambda i, f, te, tv: (i, 0)),
        scratch_shapes=[pltpu.VMEM((tm, d), F32)])
    return pl.pallas_call(
        _moe_ffn_kernel, out_shape=jax.ShapeDtypeStruct((tp, d), F32), grid_spec=grid_spec,
        compiler_params=_params("parallel", "arbitrary"), name="moe_ffn",
    )(tile_expert, tile_valid, xs, w_in, w_in, w_out, rowscale)


def _moe_layer(h, router, bias, w_in, w_out, g, b, tm=512):
    t, d = h.shape
    tm = min(tm, t)
    comb, sel = _router(h, router, bias)
    comb, sel = comb[:, :N_EXPERTS], sel[:, :N_EXPERTS] > 0.5

    seli = sel.astype(jnp.int32)
    cnt = jnp.sum(seli, axis=0)
    ntile = (cnt + tm - 1) // tm
    tile_end = jnp.cumsum(ntile)
    row_start = (tile_end - ntile) * tm
    rank = jnp.cumsum(seli, axis=0) - seli
    n_tiles = (t * TOP_K) // tm + N_EXPERTS
    tp = n_tiles * tm
    pos = jnp.where(sel, row_start[None, :] + rank, tp)
    tiles = jnp.arange(n_tiles, dtype=jnp.int32)
    tile_valid = (tiles < tile_end[-1]).astype(jnp.int32)
    tile_expert = jnp.sum((tiles[:, None] >= tile_end[None, :]).astype(jnp.int32), axis=1)
    last_e = jnp.sum((tile_end[-1] - 1 >= tile_end).astype(jnp.int32))
    tile_expert = jnp.where(tile_valid > 0, tile_expert, last_e).astype(jnp.int32)
    tok = jnp.broadcast_to(jnp.arange(t, dtype=jnp.int32)[:, None], pos.shape)
    src = jnp.zeros((tp,), jnp.int32).at[pos.reshape(-1)].set(tok.reshape(-1), mode="drop")
    rowscale = jnp.zeros((tp,), F32).at[pos.reshape(-1)].set(comb.reshape(-1), mode="drop")
    xs = jnp.take(h.astype(BF16), src, axis=0)

    ys = _moe_ffn(xs, w_in.astype(BF16), w_out.astype(BF16), tile_expert, tile_valid,
                  rowscale.reshape(tp, 1), tm)
    p_lo = jnp.min(pos, axis=1)
    p_hi = jnp.max(jnp.where(sel, pos, -1), axis=1)
    return _add_ln(h, jnp.take(ys, p_lo, axis=0), jnp.take(ys, p_hi, axis=0), g, b)


def kernel(x, mem, positions, l0_nsa_w_in, l0_nsa_cmp_pe, l0_nsa_cmp_w1, l0_nsa_cmp_w2, l0_nsa_w_out, l0_ln1_g, l0_ln1_b, l0_xq, l0_xkv, l0_xo, l0_ln2_g, l0_ln2_b, l0_ffn_w_in, l0_ffn_w_out, l0_ln3_g, l0_ln3_b, l1_mla_w_down, l1_mla_q_norm, l1_mla_kv_norm, l1_mla_w_uq, l1_mla_w_ukv, l1_mla_w_out, l1_ln1_g, l1_ln1_b, l1_xq, l1_xkv, l1_xo, l1_ln2_g, l1_ln2_b, l1_moe_router, l1_moe_bias, l1_moe_w_in, l1_moe_w_out, l1_ln3_g, l1_ln3_b, l2_moba_w_in, l2_moba_w_out, l2_ln1_g, l2_ln1_b, l2_xq, l2_xkv, l2_xo, l2_ln2_g, l2_ln2_b, l2_ffn_w_in, l2_ffn_w_out, l2_ln3_g, l2_ln3_b, l3_swa_w_in, l3_swa_sinks, l3_swa_w_out, l3_ln1_g, l3_ln1_b, l3_xq, l3_xkv, l3_xo, l3_ln2_g, l3_ln2_b, l3_moe_router, l3_moe_bias, l3_moe_w_in, l3_moe_w_out, l3_ln3_g, l3_ln3_b):
    b, s, d = x.shape
    t = b * s
    tab_p = _rope_tables(positions, ROPE_DIM, HEAD_DIM)
    tab_m = _rope_tables(positions, MLA_ROPE_DIM, MLA_ROPE_DIM)
    mem2 = mem.reshape(-1, d)

    mixers = [
        (lambda h: _nsa_mixer(h, b, s, tab_p, l0_nsa_w_in, l0_nsa_cmp_pe, l0_nsa_cmp_w1, l0_nsa_cmp_w2), l0_nsa_w_out),
        (lambda h: _mla_mixer(h, b, s, tab_m, l1_mla_w_down, l1_mla_q_norm, l1_mla_kv_norm, l1_mla_w_uq, l1_mla_w_ukv),
         l1_mla_w_out),
        (lambda h: _moba_mixer(h, b, s, tab_p, l2_moba_w_in), l2_moba_w_out),
        (lambda h: _swa_mixer(h, b, s, tab_p, l3_swa_w_in, l3_swa_sinks), l3_swa_w_out),
    ]
    ln1 = [(l0_ln1_g, l0_ln1_b), (l1_ln1_g, l1_ln1_b), (l2_ln1_g, l2_ln1_b), (l3_ln1_g, l3_ln1_b)]
    xattn = [(l0_xq, l0_xkv, l0_xo), (l1_xq, l1_xkv, l1_xo), (l2_xq, l2_xkv, l2_xo), (l3_xq, l3_xkv, l3_xo)]
    ln2 = [(l0_ln2_g, l0_ln2_b), (l1_ln2_g, l1_ln2_b), (l2_ln2_g, l2_ln2_b), (l3_ln2_g, l3_ln2_b)]
    ln3 = [(l0_ln3_g, l0_ln3_b), (l1_ln3_g, l1_ln3_b), (l2_ln3_g, l2_ln3_b), (l3_ln3_g, l3_ln3_b)]
    ffns = [
        lambda h, g, bb: _ffn_ln(h, l0_ffn_w_in, l0_ffn_w_out, g, bb),
        lambda h, g, bb: _moe_layer(h, l1_moe_router, l1_moe_bias, l1_moe_w_in, l1_moe_w_out, g, bb),
        lambda h, g, bb: _ffn_ln(h, l2_ffn_w_in, l2_ffn_w_out, g, bb),
        lambda h, g, bb: _moe_layer(h, l3_moe_router, l3_moe_bias, l3_moe_w_in, l3_moe_w_out, g, bb),
    ]

    h = x.reshape(t, d)
    for i in range(DEPTH):
        mixer, w_out = mixers[i]
        h = _mm_res_ln(mixer(h), w_out.astype(BF16), h, *ln1[i])
        w_q, w_kv, w_o = xattn[i]
        (q,) = _proj(h, [w_q.astype(BF16)], [BF16])
        (kv,) = _proj(mem2, [w_kv.astype(BF16)], [BF16])
        h = _mm_res_ln(_cross_attention(q, kv, b, s), w_o.astype(BF16), h, *ln2[i])
        h = ffns[i](h, *ln3[i])
    return h.reshape(b, s, d)
```

```python
import functools

import numpy as np
import jax
import jax.numpy as jnp
from jax import lax
from jax.experimental import pallas as pl
from jax.experimental.pallas import tpu as pltpu

BF16 = jnp.bfloat16
F32 = jnp.float32

D_MODEL = 1024
HEAD_DIM = 64
N_HEADS = D_MODEL // HEAD_DIM
ROPE_THETA = 500000.0
ROPE_DIM = HEAD_DIM // 4

NSA_KV_GROUPS = 4
NSA_CMP_BLOCK = 32
NSA_CMP_STRIDE = 16
NSA_SLC_BLOCK = 64
NSA_SLC_TOPN = 8
NSA_WINDOW = 512
NSA_FORCE_SCORE = 1e4

MLA_Q_RANK = D_MODEL // 4
MLA_KV_RANK = D_MODEL // 8
MLA_NOPE_DIM = 64
MLA_ROPE_DIM = 32
MLA_V_DIM = 64

MOBA_BLOCK = 256
MOBA_TOPK = 3

SWA_KV_HEADS = 2
SWA_WINDOW = 128

XATTN_HEADS = 4
XATTN_HEAD_DIM = D_MODEL // XATTN_HEADS

D_FF = (D_MODEL * 7) // 2
N_EXPERTS = 8
TOP_K = 2

DEPTH = 4
DEEPNORM_ALPHA = (2.0 * DEPTH) ** 0.25
LN_EPS = 1e-5
RMS_EPS = 1e-6

LANES = 128
SUBLANES = 8
NEG_INF = float("-inf")
M_INIT = -1e30
VMEM_LIMIT = 48 * 1024 * 1024


def _params(*sem):
    return pltpu.CompilerParams(dimension_semantics=sem, vmem_limit_bytes=VMEM_LIMIT)


def _ln(z, g, b):
    mu = jnp.mean(z, axis=-1, keepdims=True)
    zc = z - mu
    var = jnp.mean(zc * zc, axis=-1, keepdims=True)
    return zc * lax.rsqrt(var + LN_EPS) * g + b


def _div_pow2(x, d):
    assert d & (d - 1) == 0
    return x >> (d.bit_length() - 1)


def _is_pow2(v):
    m, _ = np.frexp(v)
    return m == 0.5


def _proj_kernel(*refs, n_out, rope, shift, rms, chunk):
    x_ref = refs[0]
    pos = 1
    if rms:
        g_ref = refs[pos]
        pos += 1
    w_refs = refs[pos:pos + n_out]
    pos += n_out
    if any(rope):
        c_ref, s1_ref, s2_ref = refs[pos:pos + 3]
        pos += 3
    o_refs = refs[pos:pos + n_out]

    x = x_ref[...]
    if rms:
        xf = x.astype(F32)
        xf = xf * lax.rsqrt(jnp.mean(xf * xf, axis=-1, keepdims=True) + RMS_EPS) * g_ref[...]
        xb = xf.astype(BF16)
    else:
        xb = x.astype(BF16)
    for i in range(n_out):
        n = w_refs[i].shape[1]
        for c0 in range(0, n, chunk):
            cw = min(chunk, n - c0)
            acc = jnp.dot(xb, w_refs[i][:, c0:c0 + cw], preferred_element_type=F32)
            if rope[i]:
                reps = cw // LANES
                c, s1, s2 = c_ref[...], s1_ref[...], s2_ref[...]
                if reps > 1:
                    c = jnp.concatenate([c] * reps, axis=1)
                    s1 = jnp.concatenate([s1] * reps, axis=1)
                    s2 = jnp.concatenate([s2] * reps, axis=1)
                acc = acc * c + pltpu.roll(acc, shift, 1) * s1 + pltpu.roll(acc, cw - shift, 1) * s2
            o_refs[i][:, c0:c0 + cw] = acc.astype(o_refs[i].dtype)


def _proj(x, ws, out_dtypes, *, rope=None, tables=None, shift=0, rms_g=None, tm=512):
    m, k = x.shape
    n_out = len(ws)
    rope = tuple(rope) if rope is not None else (False,) * n_out
    tm = min(tm, m)
    assert m % tm == 0
    in_specs = [pl.BlockSpec((tm, k), lambda i: (i, 0))]
    args = [x]
    if rms_g is not None:
        in_specs.append(pl.BlockSpec((1, k), lambda i: (0, 0)))
        args.append(rms_g.reshape(1, k).astype(F32))
    for w in ws:
        assert w.shape[1] % LANES == 0
        in_specs.append(pl.BlockSpec(w.shape, lambda i: (0, 0)))
        args.append(w)
    if any(rope):
        for t in tables:
            in_specs.append(pl.BlockSpec((tm, LANES), lambda i: (i, 0)))
            args.append(t)
    out_shape = [jax.ShapeDtypeStruct((m, w.shape[1]), dt) for w, dt in zip(ws, out_dtypes)]
    out_specs = [pl.BlockSpec((tm, w.shape[1]), lambda i: (i, 0)) for w in ws]
    kern = functools.partial(_proj_kernel, n_out=n_out, rope=rope, shift=shift,
                             rms=rms_g is not None, chunk=2 * LANES)
    return pl.pallas_call(kern, out_shape=out_shape, grid=(m // tm,), in_specs=in_specs,
                          out_specs=out_specs, compiler_params=_params("parallel"), name="proj")(*args)


def _rope_tables(positions, rot_dim, period):
    half = rot_dim // 2
    inv = ROPE_THETA ** (-jnp.arange(0, rot_dim, 2, dtype=F32) / rot_dim)
    ang = positions.astype(F32).reshape(-1)[:, None] * inv
    cos, sin = jnp.cos(ang), jnp.sin(ang)
    t = ang.shape[0]
    zero_h = jnp.zeros((t, half), F32)
    rest = period - rot_dim
    c = jnp.concatenate([cos, cos, jnp.ones((t, rest), F32)], axis=1)
    s1 = jnp.concatenate([zero_h, sin, jnp.zeros((t, rest), F32)], axis=1)
    s2 = jnp.concatenate([-sin, zero_h, jnp.zeros((t, rest), F32)], axis=1)
    reps = LANES // period
    return tuple(jnp.tile(a, (1, reps)) for a in (c, s1, s2))


def _mm_res_ln_kernel(a_ref, w_ref, h_ref, g_ref, b_ref, o_ref):
    y = jnp.dot(a_ref[...].astype(BF16), w_ref[...], preferred_element_type=F32)
    o_ref[...] = _ln(DEEPNORM_ALPHA * h_ref[...] + y, g_ref[...], b_ref[...])


def _mm_res_ln(a, w, h, g, b, tm=512):
    m, k = a.shape
    d = w.shape[1]
    tm = min(tm, m)
    return pl.pallas_call(
        _mm_res_ln_kernel,
        out_shape=jax.ShapeDtypeStruct((m, d), F32),
        grid=(m // tm,),
        in_specs=[pl.BlockSpec((tm, k), lambda i: (i, 0)),
                  pl.BlockSpec((k, d), lambda i: (0, 0)),
                  pl.BlockSpec((tm, d), lambda i: (i, 0)),
                  pl.BlockSpec((1, d), lambda i: (0, 0)),
                  pl.BlockSpec((1, d), lambda i: (0, 0))],
        out_specs=pl.BlockSpec((tm, d), lambda i: (i, 0)),
        compiler_params=_params("parallel"), name="mm_res_ln",
    )(a, w, h, g.reshape(1, d), b.reshape(1, d))


def _combine_ln_kernel(h_ref, a_ref, b2_ref, w_ref, g_ref, b_ref, o_ref):
    w = w_ref[...]
    y = w[:, 0:1] * a_ref[...].astype(F32) + w[:, 1:2] * b2_ref[...].astype(F32)
    o_ref[...] = _ln(DEEPNORM_ALPHA * h_ref[...] + y, g_ref[...], b_ref[...])


def _combine_ln(h, y2, w2, g, b, tm=512):
    m, d = h.shape
    tm = min(tm, m)
    nt = m // tm
    row = pl.BlockSpec((tm, d), lambda i: (i, 0))
    vec = pl.BlockSpec((1, d), lambda i: (0, 0))
    return pl.pallas_call(
        _combine_ln_kernel, out_shape=jax.ShapeDtypeStruct((m, d), F32), grid=(nt,),
        in_specs=[row, row, pl.BlockSpec((tm, d), lambda i: (nt + i, 0)),
                  pl.BlockSpec((tm, w2.shape[1]), lambda i: (i, 0)), vec, vec],
        out_specs=row, compiler_params=_params("parallel"), name="combine_ln",
    )(h, y2, y2, w2, g.reshape(1, d), b.reshape(1, d))


def _flash_t(qts, k_refs, vt_refs, q0, *, tq, tk, scale, window=None, m0s=None, bias_refs=None, blk=None):
    n = len(qts)
    dv = vt_refs[0].shape[1]
    fold = _is_pow2(scale)
    if fold:
        qts = [q * jnp.asarray(scale, q.dtype) for q in qts]
    qpos = q0 + lax.broadcasted_iota(jnp.int32, (1, tq), 1)
    carry = []
    for c in range(n):
        if m0s is None:
            carry += [jnp.full((1, tq), M_INIT, F32), jnp.zeros((1, tq), F32)]
        else:
            carry += [m0s[c], jnp.ones((1, tq), F32)]
        carry.append(jnp.zeros((dv, tq), F32))

    def make_body(masked):
        def body(j, carry):
            start = pl.multiple_of(j * tk, tk)
            if masked:
                delta = qpos - (start + lax.broadcasted_iota(jnp.int32, (tk, 1), 0))
                mask = delta >= 0
                if window is not None:
                    mask = mask & (delta < window)
            k_blocks, v_blocks, biases = {}, {}, {}
            out = []
            for c in range(n):
                m, l, acc = carry[3 * c:3 * c + 3]
                if id(k_refs[c]) not in k_blocks:
                    k_blocks[id(k_refs[c])] = k_refs[c][pl.ds(start, tk), :]
                    v_blocks[id(k_refs[c])] = vt_refs[c][j]
                s = jnp.dot(k_blocks[id(k_refs[c])], qts[c], preferred_element_type=F32)
                if not fold:
                    s = s * scale
                if bias_refs is not None:
                    br = bias_refs[c]
                    if id(br) not in biases:
                        per = tk // blk
                        bias = br[j, per - 1:per, :]
                        if per > 1:
                            sub = lax.broadcasted_iota(jnp.int32, (tk, tq), 0)
                            bias = jnp.broadcast_to(bias, (tk, tq))
                            for r in range(per - 2, -1, -1):
                                bias = jnp.where(sub < (r + 1) * blk, br[j, r:r + 1, :], bias)
                        biases[id(br)] = bias
                    s = s + biases[id(br)]
                if masked:
                    s = jnp.where(mask, s, NEG_INF)
                m_new = jnp.maximum(m, jnp.max(s, axis=0, keepdims=True))
                a = jnp.exp(m - m_new)
                p = jnp.exp(s - m_new)
                l_new = a * l + jnp.sum(p, axis=0, keepdims=True)
                acc_new = a * acc + jnp.dot(v_blocks[id(k_refs[c])], p.astype(BF16), preferred_element_type=F32)
                out += [m_new, l_new, acc_new]
            return tuple(out)
        return body

    diag = _div_pow2(q0, tk)
    hi = _div_pow2(q0 + tq - 1, tk)
    carry = tuple(carry)
    if window is None:
        lo_full = 0
    else:
        lo = _div_pow2(jnp.maximum(q0 - (window - 1), 0), tk)
        lo_full = jnp.minimum(_div_pow2(jnp.maximum(q0 + tq - window + tk - 1, 0), tk), diag)
        carry = lax.fori_loop(lo, lo_full, make_body(True), carry)
    carry = lax.fori_loop(lo_full, diag, make_body(False), carry)
    carry = lax.fori_loop(diag, hi + 1, make_body(True), carry)
    return [carry[3 * c + 2] / jnp.maximum(carry[3 * c + 1], 1e-30) for c in range(n)]


def _topk_rows(score, valid, k):
    row = lax.broadcasted_iota(jnp.int32, score.shape, 0)
    rank = jnp.zeros(score.shape, F32)
    for j in range(score.shape[0]):
        r = score[j:j + 1, :]
        beats = (r > score) | ((r == score) & (row > j))
        rank = rank + jnp.where(beats, 1.0, 0.0)
    return jnp.where(valid & (rank < k), 1.0, 0.0)


def _fill_bias(bias_ref, sel, per):
    bias = jnp.where(sel > 0.5, 0.0, NEG_INF)
    for jb in range(bias_ref.shape[0]):
        bias_ref[jb, 0:per, :] = bias[jb * per:(jb + 1) * per, :]


def _attn_kernel(*refs, hb, rep, tq, tk, scale, window, has_sink):
    if has_sink:
        qt_ref, k_ref, vt_ref, sink_ref, o_ref = refs
    else:
        qt_ref, k_ref, vt_ref, o_ref = refs
    q0 = pl.program_id(2) * tq
    heads = [(hh, r) for hh in range(hb) for r in range(rep)]
    k_views = [k_ref.at[hh] for hh in range(hb)]
    vt_views = [vt_ref.at[hh] for hh in range(hb)]
    outs = _flash_t([qt_ref[hh, r] for hh, r in heads], [k_views[hh] for hh, _ in heads],
                    [vt_views[hh] for hh, _ in heads], q0, tq=tq, tk=tk, scale=scale, window=window,
                    m0s=[sink_ref[hh, r] for hh, r in heads] if has_sink else None)
    for (hh, r), o in zip(heads, outs):
        o_ref[hh, r] = o.astype(o_ref.dtype)


def _attention(qt, k, vt, *, scale, window=None, sinks=None, tq=128, hb=1):
    b, g, rep, dk, s = qt.shape
    nblk, dv, tk = vt.shape[2:]
    tq = min(tq, s)
    assert g % hb == 0
    in_specs = [pl.BlockSpec((None, hb, rep, dk, tq), lambda bi, gi, i: (bi, gi, 0, 0, i)),
                pl.BlockSpec((None, hb, s, dk), lambda bi, gi, i: (bi, gi, 0, 0)),
                pl.BlockSpec((None, hb, nblk, dv, tk), lambda bi, gi, i: (bi, gi, 0, 0, 0))]
    args = [qt, k, vt]
    if sinks is not None:
        in_specs.append(pl.BlockSpec((hb, rep, 1, tq), lambda bi, gi, i: (gi, 0, 0, 0)))
        args.append(sinks)
    kern = functools.partial(_attn_kernel, hb=hb, rep=rep, tq=tq, tk=tk, scale=scale, window=window,
                             has_sink=sinks is not None)
    return pl.pallas_call(
        kern, out_shape=jax.ShapeDtypeStruct((b, g, rep, dv, s), BF16), grid=(b, g // hb, s // tq),
        in_specs=in_specs,
        out_specs=pl.BlockSpec((None, hb, rep, dv, tq), lambda bi, gi, i: (bi, gi, 0, 0, i)),
        compiler_params=_params("parallel", "parallel", "parallel"), name="attention",
    )(*args)


def _heads_first(t, b, s, h):
    return t.reshape(b, s, h, t.shape[-1] // h).transpose(0, 2, 1, 3)


def _heads_t(t, b, s, h):
    return t.reshape(b, s, h, t.shape[-1] // h).transpose(0, 2, 3, 1)


def _vt_blocks(t, b, s, h, tk):
    return t.reshape(b, s // tk, tk, h, t.shape[-1] // h).transpose(0, 3, 1, 4, 2)


def _tokens_major(ot, b, s):
    return ot.reshape(b, -1, s).transpose(0, 2, 1).reshape(b * s, -1)


def _gelu_tanh(x):
    return 0.5 * x * (1.0 + jnp.tanh(np.sqrt(2.0 / np.pi).astype(np.float32) * (x + 0.044715 * (x * x * x))))


def _nsa_compress_kernel(ak_ref, av_ref, pe_ref, w1_ref, w2_ref, kc_ref, vc_ref):
    half = ak_ref.shape[-1]
    ncp = ak_ref.shape[0]
    for j, (a_ref, o_ref) in enumerate(((ak_ref, kc_ref), (av_ref, vc_ref))):
        a = a_ref[...].astype(F32)
        lo = (a + pe_ref[j, 0:1, :]).astype(BF16)
        hi = (a + pe_ref[j, 1:2, :]).astype(BF16)
        h1 = jnp.dot(lo, w1_ref[j, :half, :], preferred_element_type=F32)
        h2 = jnp.dot(hi, w1_ref[j, half:, :], preferred_element_type=F32)
        h = _gelu_tanh(h1 + pltpu.roll(h2, ncp - 1, 0))
        o_ref[...] = jnp.dot(h.astype(BF16), w2_ref[j], preferred_element_type=F32).astype(o_ref.dtype)


def _nsa_compress(ak, av, pe2, w1, w2):
    b, g, ncp, wid = ak.shape
    dh = w2.shape[-1]
    blk = pl.BlockSpec((None, None, ncp, wid), lambda bi, gi: (bi, gi, 0, 0))
    full = lambda a: pl.BlockSpec(a.shape, lambda bi, gi: (0,) * a.ndim)
    out = pl.BlockSpec((None, None, ncp, dh), lambda bi, gi: (bi, gi, 0, 0))
    return pl.pallas_call(
        _nsa_compress_kernel,
        out_shape=[jax.ShapeDtypeStruct((b, g, ncp, dh), BF16)] * 2,
        grid=(b, g), in_specs=[blk, blk, full(pe2), full(w1), full(w2)], out_specs=[out, out],
        compiler_params=_params("parallel", "parallel"), name="nsa_compress",
    )(ak, av, pe2, w1, w2)


def _nsa_attn_kernel(qt_ref, kc_ref, vct_ref, ks_ref, vst_ref, kw_ref, vwt_ref, gl_ref, ovlt_ref, o_ref,
                     bias_ref, *, rep, tq, tk, nc, n_sel, scale):
    q0 = pl.program_id(2) * tq
    rows = rep * tq
    qts = [qt_ref[r] for r in range(rep)]
    qt = jnp.concatenate(qts, axis=1)
    row_pos = q0 + (lax.broadcasted_iota(jnp.int32, (1, rows), 1) & (tq - 1))

    ncp = kc_ref.shape[0]
    s = jnp.dot(kc_ref[...], qt * jnp.asarray(scale, qt.dtype), preferred_element_type=F32)
    cidx = lax.broadcasted_iota(jnp.int32, (ncp, 1), 0)
    cmask = (cidx * NSA_CMP_STRIDE + (NSA_CMP_BLOCK - 1) <= row_pos) & (cidx < nc)
    s = jnp.where(cmask, s, NEG_INF)
    m = jnp.max(s, axis=0, keepdims=True)
    m = jnp.where(m == NEG_INF, 0.0, m)
    e = jnp.exp(s - m)
    p = e / jnp.maximum(jnp.sum(e, axis=0, keepdims=True), 1e-30)
    o_cmp = jnp.dot(vct_ref[...], p.astype(BF16), preferred_element_type=F32)

    psum = p[:, 0:tq]
    for r in range(1, rep):
        psum = psum + p[:, r * tq:(r + 1) * tq]
    p_hi = psum.astype(BF16)
    p_lo = (psum - p_hi.astype(F32)).astype(BF16)
    ovlt = ovlt_ref[...]
    imp = (jnp.dot(ovlt, p_hi, preferred_element_type=F32)
           + jnp.dot(ovlt, p_lo, preferred_element_type=F32))
    nb = imp.shape[0]
    blk = lax.broadcasted_iota(jnp.int32, (nb, tq), 0)
    cur = _div_pow2(q0 + lax.broadcasted_iota(jnp.int32, (nb, tq), 1), NSA_SLC_BLOCK)
    forced = (blk == 0) | (blk == cur) | (blk == cur - 1)
    valid = blk <= cur
    score = jnp.where(valid, jnp.where(forced, NSA_FORCE_SCORE, imp), NEG_INF)
    _fill_bias(bias_ref, _topk_rows(score, valid, n_sel), tk // NSA_SLC_BLOCK)

    o_slc = _flash_t(qts, [ks_ref] * rep, [vst_ref] * rep, q0, tq=tq, tk=tk, scale=scale,
                     bias_refs=[bias_ref] * rep, blk=NSA_SLC_BLOCK)
    o_win = _flash_t(qts, [kw_ref] * rep, [vwt_ref] * rep, q0, tq=tq, tk=tk, scale=scale, window=NSA_WINDOW)

    for r in range(rep):
        gate = jax.nn.sigmoid(gl_ref[r])
        o = (gate[0:1, :] * o_cmp[:, r * tq:(r + 1) * tq] + gate[1:2, :] * o_slc[r] + gate[2:3, :] * o_win[r])
        o_ref[r] = o.astype(o_ref.dtype)


def _nsa_attention(qt, kc, vct, ks, vst, kw, vwt, gl, ovlt, *, nc, tq=128):
    b, g, rep, dh, s = qt.shape
    ncp = kc.shape[2]
    nblk, _, tk = vst.shape[2:]
    nb = s // NSA_SLC_BLOCK
    tq = min(tq, s)
    qspec = pl.BlockSpec((None, None, rep, dh, tq), lambda bi, gi, i: (bi, gi, 0, 0, i))
    kspec = pl.BlockSpec((None, None, s, dh), lambda bi, gi, i: (bi, gi, 0, 0))
    vspec = pl.BlockSpec((None, None, nblk, dh, tk), lambda bi, gi, i: (bi, gi, 0, 0, 0))
    kern = functools.partial(_nsa_attn_kernel, rep=rep, tq=tq, tk=tk, nc=nc,
                             n_sel=min(NSA_SLC_TOPN, nb), scale=dh ** -0.5)
    return pl.pallas_call(
        kern, out_shape=jax.ShapeDtypeStruct((b, g, rep, dh, s), BF16), grid=(b, g, s // tq),
        in_specs=[qspec,
                  pl.BlockSpec((None, None, ncp, dh), lambda bi, gi, i: (bi, gi, 0, 0)),
                  pl.BlockSpec((None, None, dh, ncp), lambda bi, gi, i: (bi, gi, 0, 0)),
                  kspec, vspec, kspec, vspec,
                  pl.BlockSpec((None, None, rep, gl.shape[3], tq), lambda bi, gi, i: (bi, gi, 0, 0, i)),
                  pl.BlockSpec(ovlt.shape, lambda bi, gi, i: (0, 0))],
        out_specs=qspec,
        scratch_shapes=[pltpu.VMEM((nblk, SUBLANES, tq), F32)],
        compiler_params=_params("parallel", "parallel", "parallel"), name="nsa_attention",
    )(qt, kc, vct, ks, vst, kw, vwt, gl, ovlt)


def _nsa_mixer(h, b, s, tab_p, w_in, cmp_pe, cmp_w1, cmp_w2):
    hn, g, dh = N_HEADS, NSA_KV_GROUPS, HEAD_DIM
    rep = hn // g
    gd = g * dh
    q_end = hn * dh
    tk = min(128, s)
    kv = lambda j: w_in[:, q_end + j * gd:q_end + (j + 1) * gd]
    w_rope = jnp.concatenate([w_in[:, :q_end], kv(0), kv(2), kv(4)], axis=1).astype(BF16)
    w_plain = jnp.concatenate([kv(1), kv(3), kv(5)], axis=1).astype(BF16)
    n_gate = 3 * hn
    w_gate = jnp.pad(w_in[:, q_end + 6 * gd:], ((0, 0), (0, LANES - n_gate))).astype(BF16)
    roped, plain, gates = _proj(h, [w_rope, w_plain, w_gate], [BF16, BF16, F32],
                                rope=[True, False, False], tables=tab_p, shift=ROPE_DIM // 2)
    qt = _heads_t(roped[:, :q_end], b, s, hn).reshape(b, g, rep, dh, s)
    k_c, k_s, k_w = [_heads_first(roped[:, q_end + j * gd:q_end + (j + 1) * gd], b, s, g) for j in range(3)]
    v_c = _heads_first(plain[:, :gd], b, s, g)
    vst, vwt = [_vt_blocks(plain[:, j * gd:(j + 1) * gd], b, s, g, tk) for j in (1, 2)]
    gl = gates[:, :n_gate].reshape(b, s, 3, g, rep).transpose(0, 3, 4, 2, 1)

    nc = (s - NSA_CMP_BLOCK) // NSA_CMP_STRIDE + 1
    ncp = s // NSA_CMP_STRIDE
    half = NSA_CMP_STRIDE * dh
    ak = k_c.reshape(b, g, ncp, half)
    av = v_c.reshape(b, g, ncp, half)
    pe2 = cmp_pe.reshape(2, 2, half).astype(F32)
    kc, vc = _nsa_compress(ak, av, pe2, cmp_w1.astype(BF16), cmp_w2.astype(BF16))
    vct = vc.transpose(0, 1, 3, 2)

    nb = s // NSA_SLC_BLOCK
    c_start = np.arange(ncp) * NSA_CMP_STRIDE
    b_start = np.arange(nb) * NSA_SLC_BLOCK
    overlap = np.clip(np.minimum(c_start[:, None] + NSA_CMP_BLOCK, b_start[None, :] + NSA_SLC_BLOCK)
                      - np.maximum(c_start[:, None], b_start[None, :]), 0, None) / NSA_CMP_BLOCK
    overlap[nc:] = 0.0
    ovlt = jnp.asarray(overlap.T, BF16)
    ot = _nsa_attention(qt, kc, vct, k_s, vst, k_w, vwt, gl, ovlt, nc=nc)
    return _tokens_major(ot, b, s)


def _mla_mixer(h, b, s, tab_m, w_down, q_norm, kv_norm, w_uq, w_ukv):
    hn = N_HEADS
    qr_, kvr = MLA_Q_RANK, MLA_KV_RANK
    w_kr = jnp.pad(w_down[:, qr_ + kvr:], ((0, 0), (0, LANES - MLA_ROPE_DIM)))
    cq, ckv, kr = _proj(h, [w_down[:, :qr_].astype(BF16), w_down[:, qr_:qr_ + kvr].astype(BF16),
                            w_kr.astype(BF16)], [F32, F32, BF16],
                        rope=[False, False, True], tables=tab_m, shift=MLA_ROPE_DIM // 2)
    w_uq3 = w_uq.reshape(qr_, hn, MLA_NOPE_DIM + MLA_ROPE_DIM)
    w_qn = w_uq3[:, :, :MLA_NOPE_DIM].reshape(qr_, hn * MLA_NOPE_DIM).astype(BF16)
    w_qr = w_uq3[:, :, MLA_NOPE_DIM:].reshape(qr_, hn * MLA_ROPE_DIM).astype(BF16)
    qn, qr = _proj(cq, [w_qn, w_qr], [BF16, BF16], rope=[False, True], tables=tab_m,
                   shift=MLA_ROPE_DIM // 2, rms_g=q_norm)
    w_ukv3 = w_ukv.reshape(kvr, hn, MLA_NOPE_DIM + MLA_V_DIM)
    w_kn = w_ukv3[:, :, :MLA_NOPE_DIM].reshape(kvr, hn * MLA_NOPE_DIM).astype(BF16)
    w_v = w_ukv3[:, :, MLA_NOPE_DIM:].reshape(kvr, hn * MLA_V_DIM).astype(BF16)
    kn, v = _proj(ckv, [w_kn, w_v], [BF16, BF16], rms_g=kv_norm)

    tk = min(256, s)
    dpad = LANES - MLA_NOPE_DIM - MLA_ROPE_DIM
    qt = jnp.concatenate([_heads_t(qn, b, s, hn), _heads_t(qr, b, s, hn),
                          jnp.zeros((b, hn, dpad, s), BF16)], axis=2)
    k_rope = jnp.broadcast_to(kr[:, :MLA_ROPE_DIM].reshape(b, 1, s, MLA_ROPE_DIM), (b, hn, s, MLA_ROPE_DIM))
    k = jnp.concatenate([_heads_first(kn, b, s, hn), k_rope, jnp.zeros((b, hn, s, dpad), BF16)], axis=-1)
    scale = (MLA_NOPE_DIM + MLA_ROPE_DIM) ** -0.5
    ot = _attention(qt[:, :, None], k, _vt_blocks(v, b, s, hn, tk), scale=scale, tq=256, hb=4)
    return _tokens_major(ot, b, s)


def _moba_kernel(qt_ref, k_ref, vt_ref, o_ref, bias_ref, *, hb, tq, n_sel, scale):
    q0 = pl.program_id(2) * tq
    s_len, dh = k_ref.shape[1:]
    nb = s_len // MOBA_BLOCK
    nbp = 2 * SUBLANES
    sub = lax.broadcasted_iota(jnp.int32, (nbp, dh), 0)
    blk = lax.broadcasted_iota(jnp.int32, (nb, tq), 0)
    own = _div_pow2(q0 + lax.broadcasted_iota(jnp.int32, (nb, tq), 1), MOBA_BLOCK)
    valid = blk < own
    qts = [qt_ref[hh] for hh in range(hb)]
    for hh in range(hb):
        k_mean = jnp.zeros((nbp, dh), F32)
        for j in range(nb):
            kj = k_ref[hh, j * MOBA_BLOCK:(j + 1) * MOBA_BLOCK, :].astype(F32)
            k_mean = jnp.where(sub == j, jnp.sum(kj, axis=0, keepdims=True) * (1.0 / MOBA_BLOCK), k_mean)
        km_hi = k_mean.astype(BF16)
        km_lo = (k_mean - km_hi.astype(F32)).astype(BF16)
        gate = (jnp.dot(km_hi, qts[hh], preferred_element_type=F32)
                + jnp.dot(km_lo, qts[hh], preferred_element_type=F32))[:nb]
        score = jnp.where(valid, gate, NEG_INF)
        sel = jnp.maximum(_topk_rows(score, valid, n_sel), jnp.where(blk == own, 1.0, 0.0))
        _fill_bias(bias_ref.at[hh], sel, 1)
    outs = _flash_t(qts, [k_ref.at[hh] for hh in range(hb)], [vt_ref.at[hh] for hh in range(hb)], q0,
                    tq=tq, tk=MOBA_BLOCK, scale=scale,
                    bias_refs=[bias_ref.at[hh] for hh in range(hb)], blk=MOBA_BLOCK)
    for hh in range(hb):
        o_ref[hh] = outs[hh].astype(o_ref.dtype)


def _moba_attention(qt, k, vt, tq=256, hb=4):
    b, hn, dh, s = qt.shape
    nb = s // MOBA_BLOCK
    n_sel = max(1, min(MOBA_TOPK, nb - 1))
    tq = min(tq, s)
    assert nb <= 2 * SUBLANES and hn % hb == 0
    qspec = pl.BlockSpec((None, hb, dh, tq), lambda bi, hi, i: (bi, hi, 0, i))
    kern = functools.partial(_moba_kernel, hb=hb, tq=tq, n_sel=n_sel, scale=dh ** -0.5)
    return pl.pallas_call(
        kern, out_shape=jax.ShapeDtypeStruct((b, hn, dh, s), BF16), grid=(b, hn // hb, s // tq),
        in_specs=[qspec,
                  pl.BlockSpec((None, hb, s, dh), lambda bi, hi, i: (bi, hi, 0, 0)),
                  pl.BlockSpec((None, hb, nb, dh, MOBA_BLOCK), lambda bi, hi, i: (bi, hi, 0, 0, 0))],
        out_specs=qspec,
        scratch_shapes=[pltpu.VMEM((hb, nb, SUBLANES, tq), F32)],
        compiler_params=_params("parallel", "parallel", "parallel"), name="moba_attention",
    )(qt, k, vt)


def _moba_mixer(h, b, s, tab_p, w_in):
    hn, dh = N_HEADS, HEAD_DIM
    assert s % MOBA_BLOCK == 0
    qk, v = _proj(h, [w_in[:, :2 * hn * dh].astype(BF16), w_in[:, 2 * hn * dh:].astype(BF16)], [BF16, BF16],
                  rope=[True, False], tables=tab_p, shift=ROPE_DIM // 2)
    qt = _heads_t(qk[:, :hn * dh], b, s, hn)
    k = _heads_first(qk[:, hn * dh:], b, s, hn)
    ot = _moba_attention(qt, k, _vt_blocks(v, b, s, hn, MOBA_BLOCK))
    return _tokens_major(ot, b, s)


def _swa_mixer(h, b, s, tab_p, w_in, sinks):
    hn, g, dh = N_HEADS, SWA_KV_HEADS, HEAD_DIM
    rep = hn // g
    tq = min(128, s)
    qk, v = _proj(h, [w_in[:, :(hn + g) * dh].astype(BF16), w_in[:, (hn + g) * dh:].astype(BF16)], [BF16, BF16],
                  rope=[True, False], tables=tab_p, shift=ROPE_DIM // 2)
    qt = _heads_t(qk[:, :hn * dh], b, s, hn).reshape(b, g, rep, dh, s)
    k = _heads_first(qk[:, hn * dh:], b, s, g)
    sink_rows = jnp.broadcast_to(sinks.astype(F32).reshape(g, rep, 1, 1), (g, rep, 1, tq))
    ot = _attention(qt, k, _vt_blocks(v, b, s, g, tq), scale=dh ** -0.5, window=SWA_WINDOW,
                    sinks=sink_rows, tq=tq)
    return _tokens_major(ot, b, s)


def _xattn_kernel(q_ref, kv_ref, o_ref, *, scale):
    d = q_ref.shape[-1]
    hd = XATTN_HEAD_DIM
    for hh in range(XATTN_HEADS):
        q = q_ref[:, hh * hd:(hh + 1) * hd]
        k = kv_ref[:, hh * hd:(hh + 1) * hd]
        v = kv_ref[:, d + hh * hd:d + (hh + 1) * hd]
        s = lax.dot_general(q, k, (((1,), (1,)), ((), ())), preferred_element_type=F32) * scale
        m = jnp.max(s, axis=-1, keepdims=True)
        e = jnp.exp(s - m)
        p = e / jnp.sum(e, axis=-1, keepdims=True)
        o_ref[:, hh * hd:(hh + 1) * hd] = jnp.dot(p.astype(BF16), v, preferred_element_type=F32).astype(o_ref.dtype)


def _cross_attention(q, kv, b, s, tq=512):
    d = q.shape[-1]
    mlen = kv.shape[0] // b
    tq = min(tq, s)
    q3 = q.reshape(b, s, d)
    kv3 = kv.reshape(b, mlen, 2 * d)
    o = pl.pallas_call(
        functools.partial(_xattn_kernel, scale=XATTN_HEAD_DIM ** -0.5),
        out_shape=jax.ShapeDtypeStruct((b, s, d), BF16), grid=(b, s // tq),
        in_specs=[pl.BlockSpec((None, tq, d), lambda bi, i: (bi, i, 0)),
                  pl.BlockSpec((None, mlen, 2 * d), lambda bi, i: (bi, 0, 0))],
        out_specs=pl.BlockSpec((None, tq, d), lambda bi, i: (bi, i, 0)),
        compiler_params=_params("parallel", "parallel"), name="cross_attention",
    )(q3, kv3)
    return o.reshape(b * s, d)


def _swiglu_partial(xb, wg_ref, wu_ref, wo_ref):
    gte = jnp.dot(xb, wg_ref[...], preferred_element_type=F32)
    up = jnp.dot(xb, wu_ref[...], preferred_element_type=F32)
    act = gte * jax.nn.sigmoid(gte) * up
    return jnp.dot(act.astype(BF16), wo_ref[...], preferred_element_type=F32)


def _ffn_ln_kernel(x_ref, wg_ref, wu_ref, wo_ref, g_ref, b_ref, o_ref, acc_ref):
    f = pl.program_id(1)

    @pl.when(f == 0)
    def _():
        acc_ref[...] = jnp.zeros_like(acc_ref)

    acc_ref[...] += _swiglu_partial(x_ref[...].astype(BF16), wg_ref, wu_ref, wo_ref)

    @pl.when(f == pl.num_programs(1) - 1)
    def _():
        o_ref[...] = _ln(DEEPNORM_ALPHA * x_ref[...] + acc_ref[...], g_ref[...], b_ref[...])


def _ffn_ln(h, w_in, w_out, g, b, tm=1024, tf=512):
    m, d = h.shape
    dff = w_out.shape[0]
    tm = min(tm, m)
    nf = dff // tf
    row = pl.BlockSpec((tm, d), lambda i, f: (i, 0))
    vec = pl.BlockSpec((1, d), lambda i, f: (0, 0))
    return pl.pallas_call(
        _ffn_ln_kernel, out_shape=jax.ShapeDtypeStruct((m, d), F32), grid=(m // tm, nf),
        in_specs=[row,
                  pl.BlockSpec((d, tf), lambda i, f: (0, f)),
                  pl.BlockSpec((d, tf), lambda i, f: (0, nf + f)),
                  pl.BlockSpec((tf, d), lambda i, f: (f, 0)),
                  vec, vec],
        out_specs=row, scratch_shapes=[pltpu.VMEM((tm, d), F32)],
        compiler_params=_params("parallel", "arbitrary"), name="ffn_ln",
    )(h, w_in.astype(BF16), w_in.astype(BF16), w_out.astype(BF16), g.reshape(1, d), b.reshape(1, d))


def _router_kernel(x_ref, rh_ref, rl_ref, b_ref, comb_ref, sel_ref):
    x = x_ref[...]
    xh = x.astype(BF16)
    xl = (x - xh.astype(F32)).astype(BF16)
    rh, rl = rh_ref[...], rl_ref[...]
    lg = (jnp.dot(xh, rh, preferred_element_type=F32) + jnp.dot(xl, rh, preferred_element_type=F32)
          + jnp.dot(xh, rl, preferred_element_type=F32)) + b_ref[...]
    lane = lax.broadcasted_iota(jnp.int32, lg.shape, 1).astype(F32)
    lg = jnp.where(lane < N_EXPERTS, lg, NEG_INF)
    m1 = jnp.max(lg, axis=-1, keepdims=True)
    i1 = jnp.min(jnp.where(lg == m1, lane, float(LANES)), axis=-1, keepdims=True)
    lg2 = jnp.where(lane == i1, NEG_INF, lg)
    m2 = jnp.max(lg2, axis=-1, keepdims=True)
    i2 = jnp.min(jnp.where(lg2 == m2, lane, float(LANES)), axis=-1, keepdims=True)
    e2 = jnp.exp(m2 - m1)
    den = 1.0 + e2
    comb_ref[...] = jnp.where(lane == i1, 1.0 / den, 0.0) + jnp.where(lane == i2, e2 / den, 0.0)
    sel_ref[...] = jnp.where((lane == i1) | (lane == i2), 1.0, 0.0)


def _router(h, router, bias, tm=512):
    m, d = h.shape
    tm = min(tm, m)
    rp = jnp.pad(router.astype(F32), ((0, 0), (0, LANES - N_EXPERTS)))
    rh = rp.astype(BF16)
    rl = (rp - rh.astype(F32)).astype(BF16)
    bp = jnp.pad(bias.astype(F32), (0, LANES - N_EXPERTS)).reshape(1, LANES)
    row = pl.BlockSpec((tm, d), lambda i: (i, 0))
    wsp = pl.BlockSpec((d, LANES), lambda i: (0, 0))
    osp = pl.BlockSpec((tm, LANES), lambda i: (i, 0))
    return pl.pallas_call(
        _router_kernel, out_shape=[jax.ShapeDtypeStruct((m, LANES), F32)] * 2, grid=(m // tm,),
        in_specs=[row, wsp, wsp, pl.BlockSpec((1, LANES), lambda i: (0, 0))], out_specs=[osp, osp],
        compiler_params=_params("parallel"), name="moe_router",
    )(h, rh, rl, bp)


def _slot_tokens_kernel(pos_ref, src_ref):
    def zero(i, c):
        src_ref[i] = 0
        return c

    lax.fori_loop(0, src_ref.shape[0], zero, 0)

    def body(i, c):
        src_ref[pos_ref[2 * i]] = i
        src_ref[pos_ref[2 * i + 1]] = i
        return c

    lax.fori_loop(0, pos_ref.shape[0] // 2, body, 0)


def _slot_tokens(pos2, tp):
    smem = pl.BlockSpec(memory_space=pltpu.SMEM)
    return pl.pallas_call(
        _slot_tokens_kernel, out_shape=jax.ShapeDtypeStruct((tp,), jnp.int32),
        in_specs=[smem], out_specs=smem, name="moe_slot_tokens",
    )(pos2)


def _moe_ffn_kernel(te_ref, tv_ref, x_ref, wg_ref, wu_ref, wo_ref, o_ref, acc_ref):
    i = pl.program_id(0)
    f = pl.program_id(1)

    @pl.when(f == 0)
    def _():
        acc_ref[...] = jnp.zeros_like(acc_ref)

    @pl.when(tv_ref[i] > 0)
    def _():
        acc_ref[...] += _swiglu_partial(x_ref[...], wg_ref, wu_ref, wo_ref)

    @pl.when(f == pl.num_programs(1) - 1)
    def _():
        o_ref[...] = acc_ref[...].astype(o_ref.dtype)


def _moe_ffn(xs, w_in, w_out, tile_expert, tile_valid, tm, tf=512):
    tp, d = xs.shape
    dff = w_out.shape[1]
    nf = dff // tf
    last = nf - 1

    def fblk(f, tv, i):
        return jnp.where(tv[i] > 0, f, last)

    grid_spec = pltpu.PrefetchScalarGridSpec(
        num_scalar_prefetch=2, grid=(tp // tm, nf),
        in_specs=[pl.BlockSpec((tm, d), lambda i, f, te, tv: (i, 0)),
                  pl.BlockSpec((None, d, tf), lambda i, f, te, tv: (te[i], 0, fblk(f, tv, i))),
                  pl.BlockSpec((None, d, tf), lambda i, f, te, tv: (te[i], 0, nf + fblk(f, tv, i))),
                  pl.BlockSpec((None, tf, d), lambda i, f, te, tv: (te[i], fblk(f, tv, i), 0))],
        out_specs=pl.BlockSpec((tm, d), lambda i, f, te, tv: (i, 0)),
        scratch_shapes=[pltpu.VMEM((tm, d), F32)])
    return pl.pallas_call(
        _moe_ffn_kernel, out_shape=jax.ShapeDtypeStruct((tp, d), BF16), grid_spec=grid_spec,
        compiler_params=_params("parallel", "arbitrary"), name="moe_ffn",
    )(tile_expert, tile_valid, xs, w_in, w_in, w_out)


def _moe_layer(h, router, bias, w_in, w_out, g, b, tm=512):
    t, d = h.shape
    tm = min(tm, t)
    comb, sel = _router(h, router, bias)
    comb, sel = comb[:, :N_EXPERTS], sel[:, :N_EXPERTS] > 0.5

    seli = sel.astype(jnp.int32)
    cnt = jnp.sum(seli, axis=0)
    ntile = (cnt + tm - 1) // tm
    tile_end = jnp.cumsum(ntile)
    row_start = (tile_end - ntile) * tm
    rank = jnp.cumsum(seli, axis=0) - seli
    n_tiles = (t * TOP_K) // tm + N_EXPERTS
    tp = n_tiles * tm
    pos = jnp.where(sel, row_start[None, :] + rank, tp)
    tiles = jnp.arange(n_tiles, dtype=jnp.int32)
    tile_valid = (tiles < tile_end[-1]).astype(jnp.int32)
    tile_expert = jnp.sum((tiles[:, None] >= tile_end[None, :]).astype(jnp.int32), axis=1)
    last_e = jnp.sum((tile_end[-1] - 1 >= tile_end).astype(jnp.int32))
    tile_expert = jnp.where(tile_valid > 0, tile_expert, last_e).astype(jnp.int32)
    p_lo = jnp.min(pos, axis=1)
    p_hi = jnp.max(jnp.where(sel, pos, -1), axis=1)
    w2 = jnp.stack([jnp.sum(jnp.where(pos == p_lo[:, None], comb, 0.0), axis=1),
                    jnp.sum(jnp.where(pos == p_hi[:, None], comb, 0.0), axis=1)], axis=1)
    src = _slot_tokens(jnp.stack([p_lo, p_hi], axis=1).reshape(-1), tp)
    xs = jnp.take(h.astype(BF16), src, axis=0)

    ys = _moe_ffn(xs, w_in.astype(BF16), w_out.astype(BF16), tile_expert, tile_valid, tm)
    y2 = jnp.take(ys, jnp.concatenate([p_lo, p_hi]), axis=0)
    return _combine_ln(h, y2, w2, g, b)


def kernel(x, mem, positions, l0_nsa_w_in, l0_nsa_cmp_pe, l0_nsa_cmp_w1, l0_nsa_cmp_w2, l0_nsa_w_out, l0_ln1_g, l0_ln1_b, l0_xq, l0_xkv, l0_xo, l0_ln2_g, l0_ln2_b, l0_ffn_w_in, l0_ffn_w_out, l0_ln3_g, l0_ln3_b, l1_mla_w_down, l1_mla_q_norm, l1_mla_kv_norm, l1_mla_w_uq, l1_mla_w_ukv, l1_mla_w_out, l1_ln1_g, l1_ln1_b, l1_xq, l1_xkv, l1_xo, l1_ln2_g, l1_ln2_b, l1_moe_router, l1_moe_bias, l1_moe_w_in, l1_moe_w_out, l1_ln3_g, l1_ln3_b, l2_moba_w_in, l2_moba_w_out, l2_ln1_g, l2_ln1_b, l2_xq, l2_xkv, l2_xo, l2_ln2_g, l2_ln2_b, l2_ffn_w_in, l2_ffn_w_out, l2_ln3_g, l2_ln3_b, l3_swa_w_in, l3_swa_sinks, l3_swa_w_out, l3_ln1_g, l3_ln1_b, l3_xq, l3_xkv, l3_xo, l3_ln2_g, l3_ln2_b, l3_moe_router, l3_moe_bias, l3_moe_w_in, l3_moe_w_out, l3_ln3_g, l3_ln3_b):
    b, s, d = x.shape
    t = b * s
    tab_p = _rope_tables(positions, ROPE_DIM, HEAD_DIM)
    tab_m = _rope_tables(positions, MLA_ROPE_DIM, MLA_ROPE_DIM)
    mem2 = mem.reshape(-1, d)

    mixers = [
        (lambda h: _nsa_mixer(h, b, s, tab_p, l0_nsa_w_in, l0_nsa_cmp_pe, l0_nsa_cmp_w1, l0_nsa_cmp_w2), l0_nsa_w_out),
        (lambda h: _mla_mixer(h, b, s, tab_m, l1_mla_w_down, l1_mla_q_norm, l1_mla_kv_norm, l1_mla_w_uq, l1_mla_w_ukv),
         l1_mla_w_out),
        (lambda h: _moba_mixer(h, b, s, tab_p, l2_moba_w_in), l2_moba_w_out),
        (lambda h: _swa_mixer(h, b, s, tab_p, l3_swa_w_in, l3_swa_sinks), l3_swa_w_out),
    ]
    ln1 = [(l0_ln1_g, l0_ln1_b), (l1_ln1_g, l1_ln1_b), (l2_ln1_g, l2_ln1_b), (l3_ln1_g, l3_ln1_b)]
    xattn = [(l0_xq, l0_xkv, l0_xo), (l1_xq, l1_xkv, l1_xo), (l2_xq, l2_xkv, l2_xo), (l3_xq, l3_xkv, l3_xo)]
    ln2 = [(l0_ln2_g, l0_ln2_b), (l1_ln2_g, l1_ln2_b), (l2_ln2_g, l2_ln2_b), (l3_ln2_g, l3_ln2_b)]
    ln3 = [(l0_ln3_g, l0_ln3_b), (l1_ln3_g, l1_ln3_b), (l2_ln3_g, l2_ln3_b), (l3_ln3_g, l3_ln3_b)]
    ffns = [
        lambda h, g, bb: _ffn_ln(h, l0_ffn_w_in, l0_ffn_w_out, g, bb),
        lambda h, g, bb: _moe_layer(h, l1_moe_router, l1_moe_bias, l1_moe_w_in, l1_moe_w_out, g, bb),
        lambda h, g, bb: _ffn_ln(h, l2_ffn_w_in, l2_ffn_w_out, g, bb),
        lambda h, g, bb: _moe_layer(h, l3_moe_router, l3_moe_bias, l3_moe_w_in, l3_moe_w_out, g, bb),
    ]

    h = x.reshape(t, d)
    for i in range(DEPTH):
        mixer, w_out = mixers[i]
        h = _mm_res_ln(mixer(h), w_out.astype(BF16), h, *ln1[i])
        w_q, w_kv, w_o = xattn[i]
        (q,) = _proj(h, [w_q.astype(BF16)], [BF16])
        (kv,) = _proj(mem2, [w_kv.astype(BF16)], [BF16])
        h = _mm_res_ln(_cross_attention(q, kv, b, s), w_o.astype(BF16), h, *ln2[i])
        h = ffns[i](h, *ln3[i])
    return h.reshape(b, s, d)
```

```python
import functools

import numpy as np
import jax
import jax.numpy as jnp
from jax import lax
from jax.experimental import pallas as pl
from jax.experimental.pallas import tpu as pltpu

BF16 = jnp.bfloat16
F32 = jnp.float32

D_MODEL = 1024
HEAD_DIM = 64
N_HEADS = D_MODEL // HEAD_DIM
ROPE_THETA = 500000.0
ROPE_DIM = HEAD_DIM // 4

NSA_KV_GROUPS = 4
NSA_CMP_BLOCK = 32
NSA_CMP_STRIDE = 16
NSA_SLC_BLOCK = 64
NSA_SLC_TOPN = 8
NSA_WINDOW = 512
NSA_FORCE_SCORE = 1e4

MLA_Q_RANK = D_MODEL // 4
MLA_KV_RANK = D_MODEL // 8
MLA_NOPE_DIM = 64
MLA_ROPE_DIM = 32
MLA_V_DIM = 64

MOBA_BLOCK = 256
MOBA_TOPK = 3

SWA_KV_HEADS = 2
SWA_WINDOW = 128

XATTN_HEADS = 4
XATTN_HEAD_DIM = D_MODEL // XATTN_HEADS

D_FF = (D_MODEL * 7) // 2
N_EXPERTS = 8
TOP_K = 2

DEPTH = 4
DEEPNORM_ALPHA = (2.0 * DEPTH) ** 0.25
LN_EPS = 1e-5
RMS_EPS = 1e-6

LANES = 128
SUBLANES = 8
NEG_INF = float("-inf")
M_INIT = -1e30
VMEM_LIMIT = 48 * 1024 * 1024


def _params(*sem):
    return pltpu.CompilerParams(dimension_semantics=sem, vmem_limit_bytes=VMEM_LIMIT)


def _ln(z, g, b):
    mu = jnp.mean(z, axis=-1, keepdims=True)
    zc = z - mu
    var = jnp.mean(zc * zc, axis=-1, keepdims=True)
    return zc * lax.rsqrt(var + LN_EPS) * g + b


def _div_pow2(x, d):
    assert d & (d - 1) == 0
    return x >> (d.bit_length() - 1)


def _is_pow2(v):
    m, _ = np.frexp(v)
    return m == 0.5


def _proj_kernel(*refs, n_out, rope, shift, rms, chunk):
    x_ref = refs[0]
    pos = 1
    if rms:
        g_ref = refs[pos]
        pos += 1
    w_refs = refs[pos:pos + n_out]
    pos += n_out
    if any(rope):
        c_ref, s1_ref, s2_ref = refs[pos:pos + 3]
        pos += 3
    o_refs = refs[pos:pos + n_out]

    x = x_ref[...]
    if rms:
        xf = x.astype(F32)
        xf = xf * lax.rsqrt(jnp.mean(xf * xf, axis=-1, keepdims=True) + RMS_EPS) * g_ref[...]
        xb = xf.astype(BF16)
    else:
        xb = x.astype(BF16)
    for i in range(n_out):
        n = w_refs[i].shape[1]
        for c0 in range(0, n, chunk):
            cw = min(chunk, n - c0)
            acc = jnp.dot(xb, w_refs[i][:, c0:c0 + cw], preferred_element_type=F32)
            if rope[i]:
                reps = cw // LANES
                c, s1, s2 = c_ref[...], s1_ref[...], s2_ref[...]
                if reps > 1:
                    c = jnp.concatenate([c] * reps, axis=1)
                    s1 = jnp.concatenate([s1] * reps, axis=1)
                    s2 = jnp.concatenate([s2] * reps, axis=1)
                acc = acc * c + pltpu.roll(acc, shift, 1) * s1 + pltpu.roll(acc, cw - shift, 1) * s2
            o_refs[i][:, c0:c0 + cw] = acc.astype(o_refs[i].dtype)


def _proj(x, ws, out_dtypes, *, rope=None, tables=None, shift=0, rms_g=None, tm=512):
    m, k = x.shape
    n_out = len(ws)
    rope = tuple(rope) if rope is not None else (False,) * n_out
    tm = min(tm, m)
    assert m % tm == 0
    in_specs = [pl.BlockSpec((tm, k), lambda i: (i, 0))]
    args = [x]
    if rms_g is not None:
        in_specs.append(pl.BlockSpec((1, k), lambda i: (0, 0)))
        args.append(rms_g.reshape(1, k).astype(F32))
    for w in ws:
        assert w.shape[1] % LANES == 0
        in_specs.append(pl.BlockSpec(w.shape, lambda i: (0, 0)))
        args.append(w)
    if any(rope):
        for t in tables:
            in_specs.append(pl.BlockSpec((tm, LANES), lambda i: (i, 0)))
            args.append(t)
    out_shape = [jax.ShapeDtypeStruct((m, w.shape[1]), dt) for w, dt in zip(ws, out_dtypes)]
    out_specs = [pl.BlockSpec((tm, w.shape[1]), lambda i: (i, 0)) for w in ws]
    kern = functools.partial(_proj_kernel, n_out=n_out, rope=rope, shift=shift,
                             rms=rms_g is not None, chunk=2 * LANES)
    return pl.pallas_call(kern, out_shape=out_shape, grid=(m // tm,), in_specs=in_specs,
                          out_specs=out_specs, compiler_params=_params("parallel"), name="proj")(*args)


def _rope_tables(positions, rot_dim, period):
    half = rot_dim // 2
    inv = ROPE_THETA ** (-jnp.arange(0, rot_dim, 2, dtype=F32) / rot_dim)
    ang = positions.astype(F32).reshape(-1)[:, None] * inv
    cos, sin = jnp.cos(ang), jnp.sin(ang)
    t = ang.shape[0]
    zero_h = jnp.zeros((t, half), F32)
    rest = period - rot_dim
    c = jnp.concatenate([cos, cos, jnp.ones((t, rest), F32)], axis=1)
    s1 = jnp.concatenate([zero_h, sin, jnp.zeros((t, rest), F32)], axis=1)
    s2 = jnp.concatenate([-sin, zero_h, jnp.zeros((t, rest), F32)], axis=1)
    reps = LANES // period
    return tuple(jnp.tile(a, (1, reps)) for a in (c, s1, s2))


def _mm_res_ln_kernel(a_ref, w_ref, h_ref, g_ref, b_ref, o_ref):
    y = jnp.dot(a_ref[...].astype(BF16), w_ref[...], preferred_element_type=F32)
    o_ref[...] = _ln(DEEPNORM_ALPHA * h_ref[...] + y, g_ref[...], b_ref[...])


def _mm_res_ln(a, w, h, g, b, tm=512):
    m, k = a.shape
    d = w.shape[1]
    tm = min(tm, m)
    return pl.pallas_call(
        _mm_res_ln_kernel,
        out_shape=jax.ShapeDtypeStruct((m, d), F32),
        grid=(m // tm,),
        in_specs=[pl.BlockSpec((tm, k), lambda i: (i, 0)),
                  pl.BlockSpec((k, d), lambda i: (0, 0)),
                  pl.BlockSpec((tm, d), lambda i: (i, 0)),
                  pl.BlockSpec((1, d), lambda i: (0, 0)),
                  pl.BlockSpec((1, d), lambda i: (0, 0))],
        out_specs=pl.BlockSpec((tm, d), lambda i: (i, 0)),
        compiler_params=_params("parallel"), name="mm_res_ln",
    )(a, w, h, g.reshape(1, d), b.reshape(1, d))


def _combine_ln_kernel(base_ref, h_ref, pos_ref, gate_ref, ya_ref, yb_ref, g_ref, b_ref, o_ref, acc_ref):
    i = pl.program_id(0)
    e = pl.program_id(1)
    tm = h_ref.shape[0]

    @pl.when(e == 0)
    def _():
        acc_ref[...] = jnp.zeros_like(acc_ref)

    rel = pos_ref[...] - base_ref[i * N_EXPERTS + e] * tm
    lane = lax.broadcasted_iota(jnp.int32, (tm, tm), 1)
    pick_a = jnp.where(rel == lane, 1.0, 0.0).astype(BF16)
    pick_b = jnp.where(rel == lane + tm, 1.0, 0.0).astype(BF16)
    y = (jnp.dot(pick_a, ya_ref[...], preferred_element_type=F32)
         + jnp.dot(pick_b, yb_ref[...], preferred_element_type=F32))
    acc_ref[...] += gate_ref[...] * y

    @pl.when(e == N_EXPERTS - 1)
    def _():
        o_ref[...] = _ln(DEEPNORM_ALPHA * h_ref[...] + acc_ref[...], g_ref[...], b_ref[...])


def _combine_ln(h, ys, pos_e, gate_e, base, g, b, tm):
    m, d = h.shape
    row = pl.BlockSpec((tm, d), lambda i, e, base: (i, 0))
    col = pl.BlockSpec((None, tm, 1), lambda i, e, base: (e, i, 0))
    vec = pl.BlockSpec((1, d), lambda i, e, base: (0, 0))
    grid_spec = pltpu.PrefetchScalarGridSpec(
        num_scalar_prefetch=1, grid=(m // tm, N_EXPERTS),
        in_specs=[row, col, col,
                  pl.BlockSpec((tm, d), lambda i, e, base: (base[i * N_EXPERTS + e], 0)),
                  pl.BlockSpec((tm, d), lambda i, e, base: (base[i * N_EXPERTS + e] + 1, 0)),
                  vec, vec],
        out_specs=row, scratch_shapes=[pltpu.VMEM((tm, d), F32)])
    return pl.pallas_call(
        _combine_ln_kernel, out_shape=jax.ShapeDtypeStruct((m, d), F32), grid_spec=grid_spec,
        compiler_params=_params("parallel", "arbitrary"), name="combine_ln",
    )(base, h, pos_e, gate_e, ys, ys, g.reshape(1, d), b.reshape(1, d))


def _flash_t(qts, k_refs, vt_refs, q0, *, tq, tk, scale, window=None, m0s=None, bias_refs=None, blk=None):
    n = len(qts)
    dv = vt_refs[0].shape[1]
    fold = _is_pow2(scale)
    if fold:
        qts = [q * jnp.asarray(scale, q.dtype) for q in qts]
    qpos = q0 + lax.broadcasted_iota(jnp.int32, (1, tq), 1)
    carry = []
    for c in range(n):
        if m0s is None:
            carry += [jnp.full((1, tq), M_INIT, F32), jnp.zeros((1, tq), F32)]
        else:
            carry += [m0s[c], jnp.ones((1, tq), F32)]
        carry.append(jnp.zeros((dv, tq), F32))

    def make_body(masked):
        def body(j, carry):
            start = pl.multiple_of(j * tk, tk)
            if masked:
                delta = qpos - (start + lax.broadcasted_iota(jnp.int32, (tk, 1), 0))
                mask = delta >= 0
                if window is not None:
                    mask = mask & (delta < window)
            k_blocks, v_blocks, biases = {}, {}, {}
            out = []
            for c in range(n):
                m, l, acc = carry[3 * c:3 * c + 3]
                if id(k_refs[c]) not in k_blocks:
                    k_blocks[id(k_refs[c])] = k_refs[c][pl.ds(start, tk), :]
                    v_blocks[id(k_refs[c])] = vt_refs[c][j]
                s = jnp.dot(k_blocks[id(k_refs[c])], qts[c], preferred_element_type=F32)
                if not fold:
                    s = s * scale
                if bias_refs is not None:
                    br = bias_refs[c]
                    if id(br) not in biases:
                        per = tk // blk
                        bias = br[j, per - 1:per, :]
                        if per > 1:
                            sub = lax.broadcasted_iota(jnp.int32, (tk, tq), 0)
                            bias = jnp.broadcast_to(bias, (tk, tq))
                            for r in range(per - 2, -1, -1):
                                bias = jnp.where(sub < (r + 1) * blk, br[j, r:r + 1, :], bias)
                        biases[id(br)] = bias
                    s = s + biases[id(br)]
                if masked:
                    s = jnp.where(mask, s, NEG_INF)
                m_new = jnp.maximum(m, jnp.max(s, axis=0, keepdims=True))
                a = jnp.exp(m - m_new)
                p = jnp.exp(s - m_new)
                l_new = a * l + jnp.sum(p, axis=0, keepdims=True)
                acc_new = a * acc + jnp.dot(v_blocks[id(k_refs[c])], p.astype(BF16), preferred_element_type=F32)
                out += [m_new, l_new, acc_new]
            return tuple(out)
        return body

    diag = _div_pow2(q0, tk)
    hi = _div_pow2(q0 + tq - 1, tk)
    carry = tuple(carry)
    if window is None:
        lo_full = 0
    else:
        lo = _div_pow2(jnp.maximum(q0 - (window - 1), 0), tk)
        lo_full = jnp.minimum(_div_pow2(jnp.maximum(q0 + tq - window + tk - 1, 0), tk), diag)
        carry = lax.fori_loop(lo, lo_full, make_body(True), carry)
    carry = lax.fori_loop(lo_full, diag, make_body(False), carry)
    carry = lax.fori_loop(diag, hi + 1, make_body(True), carry)
    return [carry[3 * c + 2] / jnp.maximum(carry[3 * c + 1], 1e-30) for c in range(n)]


def _topk_rows(score, valid, k):
    row = lax.broadcasted_iota(jnp.int32, score.shape, 0)
    rank = jnp.zeros(score.shape, F32)
    for j in range(score.shape[0]):
        r = score[j:j + 1, :]
        beats = (r > score) | ((r == score) & (row > j))
        rank = rank + jnp.where(beats, 1.0, 0.0)
    return jnp.where(valid & (rank < k), 1.0, 0.0)


def _fill_bias(bias_ref, sel, per):
    bias = jnp.where(sel > 0.5, 0.0, NEG_INF)
    for jb in range(bias_ref.shape[0]):
        bias_ref[jb, 0:per, :] = bias[jb * per:(jb + 1) * per, :]


def _attn_kernel(*refs, hb, rep, tq, tk, scale, window, has_sink):
    if has_sink:
        qt_ref, k_ref, vt_ref, sink_ref, o_ref = refs
    else:
        qt_ref, k_ref, vt_ref, o_ref = refs
    q0 = pl.program_id(2) * tq
    heads = [(hh, r) for hh in range(hb) for r in range(rep)]
    k_views = [k_ref.at[hh] for hh in range(hb)]
    vt_views = [vt_ref.at[hh] for hh in range(hb)]
    outs = _flash_t([qt_ref[hh, r] for hh, r in heads], [k_views[hh] for hh, _ in heads],
                    [vt_views[hh] for hh, _ in heads], q0, tq=tq, tk=tk, scale=scale, window=window,
                    m0s=[sink_ref[hh, r] for hh, r in heads] if has_sink else None)
    for (hh, r), o in zip(heads, outs):
        o_ref[hh, r] = o.astype(o_ref.dtype)


def _attention(qt, k, vt, *, scale, window=None, sinks=None, tq=128, hb=1):
    b, g, rep, dk, s = qt.shape
    nblk, dv, tk = vt.shape[2:]
    tq = min(tq, s)
    assert g % hb == 0
    in_specs = [pl.BlockSpec((None, hb, rep, dk, tq), lambda bi, gi, i: (bi, gi, 0, 0, i)),
                pl.BlockSpec((None, hb, s, dk), lambda bi, gi, i: (bi, gi, 0, 0)),
                pl.BlockSpec((None, hb, nblk, dv, tk), lambda bi, gi, i: (bi, gi, 0, 0, 0))]
    args = [qt, k, vt]
    if sinks is not None:
        in_specs.append(pl.BlockSpec((hb, rep, 1, tq), lambda bi, gi, i: (gi, 0, 0, 0)))
        args.append(sinks)
    kern = functools.partial(_attn_kernel, hb=hb, rep=rep, tq=tq, tk=tk, scale=scale, window=window,
                             has_sink=sinks is not None)
    return pl.pallas_call(
        kern, out_shape=jax.ShapeDtypeStruct((b, g, rep, dv, s), BF16), grid=(b, g // hb, s // tq),
        in_specs=in_specs,
        out_specs=pl.BlockSpec((None, hb, rep, dv, tq), lambda bi, gi, i: (bi, gi, 0, 0, i)),
        compiler_params=_params("parallel", "parallel", "parallel"), name="attention",
    )(*args)


def _heads_first(t, b, s, h):
    return t.reshape(b, s, h, t.shape[-1] // h).transpose(0, 2, 1, 3)


def _heads_t(t, b, s, h):
    return t.reshape(b, s, h, t.shape[-1] // h).transpose(0, 2, 3, 1)


def _vt_blocks(t, b, s, h, tk):
    return t.reshape(b, s // tk, tk, h, t.shape[-1] // h).transpose(0, 3, 1, 4, 2)


def _tokens_major(ot, b, s):
    return ot.reshape(b, -1, s).transpose(0, 2, 1).reshape(b * s, -1)


def _gelu_tanh(x):
    return 0.5 * x * (1.0 + jnp.tanh(np.sqrt(2.0 / np.pi).astype(np.float32) * (x + 0.044715 * (x * x * x))))


def _nsa_compress_kernel(ak_ref, av_ref, pe_ref, w1_ref, w2_ref, kc_ref, vc_ref):
    half = ak_ref.shape[-1]
    ncp = ak_ref.shape[0]
    for j, (a_ref, o_ref) in enumerate(((ak_ref, kc_ref), (av_ref, vc_ref))):
        a = a_ref[...].astype(F32)
        lo = (a + pe_ref[j, 0:1, :]).astype(BF16)
        hi = (a + pe_ref[j, 1:2, :]).astype(BF16)
        h1 = jnp.dot(lo, w1_ref[j, :half, :], preferred_element_type=F32)
        h2 = jnp.dot(hi, w1_ref[j, half:, :], preferred_element_type=F32)
        h = _gelu_tanh(h1 + pltpu.roll(h2, ncp - 1, 0))
        o_ref[...] = jnp.dot(h.astype(BF16), w2_ref[j], preferred_element_type=F32).astype(o_ref.dtype)


def _nsa_compress(ak, av, pe2, w1, w2):
    b, g, ncp, wid = ak.shape
    dh = w2.shape[-1]
    blk = pl.BlockSpec((None, None, ncp, wid), lambda bi, gi: (bi, gi, 0, 0))
    full = lambda a: pl.BlockSpec(a.shape, lambda bi, gi: (0,) * a.ndim)
    out = pl.BlockSpec((None, None, ncp, dh), lambda bi, gi: (bi, gi, 0, 0))
    return pl.pallas_call(
        _nsa_compress_kernel,
        out_shape=[jax.ShapeDtypeStruct((b, g, ncp, dh), BF16)] * 2,
        grid=(b, g), in_specs=[blk, blk, full(pe2), full(w1), full(w2)], out_specs=[out, out],
        compiler_params=_params("parallel", "parallel"), name="nsa_compress",
    )(ak, av, pe2, w1, w2)


def _nsa_attn_kernel(qt_ref, kc_ref, vct_ref, ks_ref, vst_ref, kw_ref, vwt_ref, gl_ref, ovlt_ref, o_ref,
                     bias_ref, *, rep, tq, tk, nc, n_sel, scale):
    q0 = pl.program_id(2) * tq
    rows = rep * tq
    qts = [qt_ref[r] for r in range(rep)]
    qt = jnp.concatenate(qts, axis=1)
    row_pos = q0 + (lax.broadcasted_iota(jnp.int32, (1, rows), 1) & (tq - 1))

    ncp = kc_ref.shape[0]
    s = jnp.dot(kc_ref[...], qt * jnp.asarray(scale, qt.dtype), preferred_element_type=F32)
    cidx = lax.broadcasted_iota(jnp.int32, (ncp, 1), 0)
    cmask = (cidx * NSA_CMP_STRIDE + (NSA_CMP_BLOCK - 1) <= row_pos) & (cidx < nc)
    s = jnp.where(cmask, s, NEG_INF)
    m = jnp.max(s, axis=0, keepdims=True)
    m = jnp.where(m == NEG_INF, 0.0, m)
    e = jnp.exp(s - m)
    p = e / jnp.maximum(jnp.sum(e, axis=0, keepdims=True), 1e-30)
    o_cmp = jnp.dot(vct_ref[...], p.astype(BF16), preferred_element_type=F32)

    psum = p[:, 0:tq]
    for r in range(1, rep):
        psum = psum + p[:, r * tq:(r + 1) * tq]
    p_hi = psum.astype(BF16)
    p_lo = (psum - p_hi.astype(F32)).astype(BF16)
    ovlt = ovlt_ref[...]
    imp = (jnp.dot(ovlt, p_hi, preferred_element_type=F32)
           + jnp.dot(ovlt, p_lo, preferred_element_type=F32))
    nb = imp.shape[0]
    blk = lax.broadcasted_iota(jnp.int32, (nb, tq), 0)
    cur = _div_pow2(q0 + lax.broadcasted_iota(jnp.int32, (nb, tq), 1), NSA_SLC_BLOCK)
    forced = (blk == 0) | (blk == cur) | (blk == cur - 1)
    valid = blk <= cur
    score = jnp.where(valid, jnp.where(forced, NSA_FORCE_SCORE, imp), NEG_INF)
    _fill_bias(bias_ref, _topk_rows(score, valid, n_sel), tk // NSA_SLC_BLOCK)

    o_slc = _flash_t(qts, [ks_ref] * rep, [vst_ref] * rep, q0, tq=tq, tk=tk, scale=scale,
                     bias_refs=[bias_ref] * rep, blk=NSA_SLC_BLOCK)
    o_win = _flash_t(qts, [kw_ref] * rep, [vwt_ref] * rep, q0, tq=tq, tk=tk, scale=scale, window=NSA_WINDOW)

    for r in range(rep):
        gate = jax.nn.sigmoid(gl_ref[r])
        o = (gate[0:1, :] * o_cmp[:, r * tq:(r + 1) * tq] + gate[1:2, :] * o_slc[r] + gate[2:3, :] * o_win[r])
        o_ref[r] = o.astype(o_ref.dtype)


def _nsa_attention(qt, kc, vct, ks, vst, kw, vwt, gl, ovlt, *, nc, tq=128):
    b, g, rep, dh, s = qt.shape
    ncp = kc.shape[2]
    nblk, _, tk = vst.shape[2:]
    nb = s // NSA_SLC_BLOCK
    tq = min(tq, s)
    qspec = pl.BlockSpec((None, None, rep, dh, tq), lambda bi, gi, i: (bi, gi, 0, 0, i))
    kspec = pl.BlockSpec((None, None, s, dh), lambda bi, gi, i: (bi, gi, 0, 0))
    vspec = pl.BlockSpec((None, None, nblk, dh, tk), lambda bi, gi, i: (bi, gi, 0, 0, 0))
    kern = functools.partial(_nsa_attn_kernel, rep=rep, tq=tq, tk=tk, nc=nc,
                             n_sel=min(NSA_SLC_TOPN, nb), scale=dh ** -0.5)
    return pl.pallas_call(
        kern, out_shape=jax.ShapeDtypeStruct((b, g, rep, dh, s), BF16), grid=(b, g, s // tq),
        in_specs=[qspec,
                  pl.BlockSpec((None, None, ncp, dh), lambda bi, gi, i: (bi, gi, 0, 0)),
                  pl.BlockSpec((None, None, dh, ncp), lambda bi, gi, i: (bi, gi, 0, 0)),
                  kspec, vspec, kspec, vspec,
                  pl.BlockSpec((None, None, rep, gl.shape[3], tq), lambda bi, gi, i: (bi, gi, 0, 0, i)),
                  pl.BlockSpec(ovlt.shape, lambda bi, gi, i: (0, 0))],
        out_specs=qspec,
        scratch_shapes=[pltpu.VMEM((nblk, SUBLANES, tq), F32)],
        compiler_params=_params("parallel", "parallel", "parallel"), name="nsa_attention",
    )(qt, kc, vct, ks, vst, kw, vwt, gl, ovlt)


def _nsa_mixer(h, b, s, tab_p, w_in, cmp_pe, cmp_w1, cmp_w2):
    hn, g, dh = N_HEADS, NSA_KV_GROUPS, HEAD_DIM
    rep = hn // g
    gd = g * dh
    q_end = hn * dh
    tk = min(128, s)
    kv = lambda j: w_in[:, q_end + j * gd:q_end + (j + 1) * gd]
    w_rope = jnp.concatenate([w_in[:, :q_end], kv(0), kv(2), kv(4)], axis=1).astype(BF16)
    w_plain = jnp.concatenate([kv(1), kv(3), kv(5)], axis=1).astype(BF16)
    n_gate = 3 * hn
    w_gate = jnp.pad(w_in[:, q_end + 6 * gd:], ((0, 0), (0, LANES - n_gate))).astype(BF16)
    roped, plain, gates = _proj(h, [w_rope, w_plain, w_gate], [BF16, BF16, F32],
                                rope=[True, False, False], tables=tab_p, shift=ROPE_DIM // 2)
    qt = _heads_t(roped[:, :q_end], b, s, hn).reshape(b, g, rep, dh, s)
    k_c, k_s, k_w = [_heads_first(roped[:, q_end + j * gd:q_end + (j + 1) * gd], b, s, g) for j in range(3)]
    v_c = _heads_first(plain[:, :gd], b, s, g)
    vst, vwt = [_vt_blocks(plain[:, j * gd:(j + 1) * gd], b, s, g, tk) for j in (1, 2)]
    gl = gates[:, :n_gate].reshape(b, s, 3, g, rep).transpose(0, 3, 4, 2, 1)

    nc = (s - NSA_CMP_BLOCK) // NSA_CMP_STRIDE + 1
    ncp = s // NSA_CMP_STRIDE
    half = NSA_CMP_STRIDE * dh
    ak = k_c.reshape(b, g, ncp, half)
    av = v_c.reshape(b, g, ncp, half)
    pe2 = cmp_pe.reshape(2, 2, half).astype(F32)
    kc, vc = _nsa_compress(ak, av, pe2, cmp_w1.astype(BF16), cmp_w2.astype(BF16))
    vct = vc.transpose(0, 1, 3, 2)

    nb = s // NSA_SLC_BLOCK
    c_start = np.arange(ncp) * NSA_CMP_STRIDE
    b_start = np.arange(nb) * NSA_SLC_BLOCK
    overlap = np.clip(np.minimum(c_start[:, None] + NSA_CMP_BLOCK, b_start[None, :] + NSA_SLC_BLOCK)
                      - np.maximum(c_start[:, None], b_start[None, :]), 0, None) / NSA_CMP_BLOCK
    overlap[nc:] = 0.0
    ovlt = jnp.asarray(overlap.T, BF16)
    ot = _nsa_attention(qt, kc, vct, k_s, vst, k_w, vwt, gl, ovlt, nc=nc)
    return _tokens_major(ot, b, s)


def _mla_mixer(h, b, s, tab_m, w_down, q_norm, kv_norm, w_uq, w_ukv):
    hn = N_HEADS
    qr_, kvr = MLA_Q_RANK, MLA_KV_RANK
    w_kr = jnp.pad(w_down[:, qr_ + kvr:], ((0, 0), (0, LANES - MLA_ROPE_DIM)))
    cq, ckv, kr = _proj(h, [w_down[:, :qr_].astype(BF16), w_down[:, qr_:qr_ + kvr].astype(BF16),
                            w_kr.astype(BF16)], [F32, F32, BF16],
                        rope=[False, False, True], tables=tab_m, shift=MLA_ROPE_DIM // 2)
    w_uq3 = w_uq.reshape(qr_, hn, MLA_NOPE_DIM + MLA_ROPE_DIM)
    w_qn = w_uq3[:, :, :MLA_NOPE_DIM].reshape(qr_, hn * MLA_NOPE_DIM).astype(BF16)
    w_qr = w_uq3[:, :, MLA_NOPE_DIM:].reshape(qr_, hn * MLA_ROPE_DIM).astype(BF16)
    qn, qr = _proj(cq, [w_qn, w_qr], [BF16, BF16], rope=[False, True], tables=tab_m,
                   shift=MLA_ROPE_DIM // 2, rms_g=q_norm)
    w_ukv3 = w_ukv.reshape(kvr, hn, MLA_NOPE_DIM + MLA_V_DIM)
    w_kn = w_ukv3[:, :, :MLA_NOPE_DIM].reshape(kvr, hn * MLA_NOPE_DIM).astype(BF16)
    w_v = w_ukv3[:, :, MLA_NOPE_DIM:].reshape(kvr, hn * MLA_V_DIM).astype(BF16)
    kn, v = _proj(ckv, [w_kn, w_v], [BF16, BF16], rms_g=kv_norm)

    tk = min(256, s)
    dpad = LANES - MLA_NOPE_DIM - MLA_ROPE_DIM
    qt = jnp.concatenate([_heads_t(qn, b, s, hn), _heads_t(qr, b, s, hn),
                          jnp.zeros((b, hn, dpad, s), BF16)], axis=2)
    k_rope = jnp.broadcast_to(kr[:, :MLA_ROPE_DIM].reshape(b, 1, s, MLA_ROPE_DIM), (b, hn, s, MLA_ROPE_DIM))
    k = jnp.concatenate([_heads_first(kn, b, s, hn), k_rope, jnp.zeros((b, hn, s, dpad), BF16)], axis=-1)
    scale = (MLA_NOPE_DIM + MLA_ROPE_DIM) ** -0.5
    ot = _attention(qt[:, :, None], k, _vt_blocks(v, b, s, hn, tk), scale=scale, tq=256, hb=4)
    return _tokens_major(ot, b, s)


def _moba_kernel(qt_ref, k_ref, vt_ref, o_ref, bias_ref, *, hb, tq, n_sel, scale):
    q0 = pl.program_id(2) * tq
    s_len, dh = k_ref.shape[1:]
    nb = s_len // MOBA_BLOCK
    nbp = 2 * SUBLANES
    sub = lax.broadcasted_iota(jnp.int32, (nbp, dh), 0)
    blk = lax.broadcasted_iota(jnp.int32, (nb, tq), 0)
    own = _div_pow2(q0 + lax.broadcasted_iota(jnp.int32, (nb, tq), 1), MOBA_BLOCK)
    valid = blk < own
    qts = [qt_ref[hh] for hh in range(hb)]
    for hh in range(hb):
        k_mean = jnp.zeros((nbp, dh), F32)
        for j in range(nb):
            kj = k_ref[hh, j * MOBA_BLOCK:(j + 1) * MOBA_BLOCK, :].astype(F32)
            k_mean = jnp.where(sub == j, jnp.sum(kj, axis=0, keepdims=True) * (1.0 / MOBA_BLOCK), k_mean)
        km_hi = k_mean.astype(BF16)
        km_lo = (k_mean - km_hi.astype(F32)).astype(BF16)
        gate = (jnp.dot(km_hi, qts[hh], preferred_element_type=F32)
                + jnp.dot(km_lo, qts[hh], preferred_element_type=F32))[:nb]
        score = jnp.where(valid, gate, NEG_INF)
        sel = jnp.maximum(_topk_rows(score, valid, n_sel), jnp.where(blk == own, 1.0, 0.0))
        _fill_bias(bias_ref.at[hh], sel, 1)
    outs = _flash_t(qts, [k_ref.at[hh] for hh in range(hb)], [vt_ref.at[hh] for hh in range(hb)], q0,
                    tq=tq, tk=MOBA_BLOCK, scale=scale,
                    bias_refs=[bias_ref.at[hh] for hh in range(hb)], blk=MOBA_BLOCK)
    for hh in range(hb):
        o_ref[hh] = outs[hh].astype(o_ref.dtype)


def _moba_attention(qt, k, vt, tq=256, hb=4):
    b, hn, dh, s = qt.shape
    nb = s // MOBA_BLOCK
    n_sel = max(1, min(MOBA_TOPK, nb - 1))
    tq = min(tq, s)
    assert nb <= 2 * SUBLANES and hn % hb == 0
    qspec = pl.BlockSpec((None, hb, dh, tq), lambda bi, hi, i: (bi, hi, 0, i))
    kern = functools.partial(_moba_kernel, hb=hb, tq=tq, n_sel=n_sel, scale=dh ** -0.5)
    return pl.pallas_call(
        kern, out_shape=jax.ShapeDtypeStruct((b, hn, dh, s), BF16), grid=(b, hn // hb, s // tq),
        in_specs=[qspec,
                  pl.BlockSpec((None, hb, s, dh), lambda bi, hi, i: (bi, hi, 0, 0)),
                  pl.BlockSpec((None, hb, nb, dh, MOBA_BLOCK), lambda bi, hi, i: (bi, hi, 0, 0, 0))],
        out_specs=qspec,
        scratch_shapes=[pltpu.VMEM((hb, nb, SUBLANES, tq), F32)],
        compiler_params=_params("parallel", "parallel", "parallel"), name="moba_attention",
    )(qt, k, vt)


def _moba_mixer(h, b, s, tab_p, w_in):
    hn, dh = N_HEADS, HEAD_DIM
    assert s % MOBA_BLOCK == 0
    qk, v = _proj(h, [w_in[:, :2 * hn * dh].astype(BF16), w_in[:, 2 * hn * dh:].astype(BF16)], [BF16, BF16],
                  rope=[True, False], tables=tab_p, shift=ROPE_DIM // 2)
    qt = _heads_t(qk[:, :hn * dh], b, s, hn)
    k = _heads_first(qk[:, hn * dh:], b, s, hn)
    ot = _moba_attention(qt, k, _vt_blocks(v, b, s, hn, MOBA_BLOCK))
    return _tokens_major(ot, b, s)


def _swa_mixer(h, b, s, tab_p, w_in, sinks):
    hn, g, dh = N_HEADS, SWA_KV_HEADS, HEAD_DIM
    rep = hn // g
    tq = min(128, s)
    qk, v = _proj(h, [w_in[:, :(hn + g) * dh].astype(BF16), w_in[:, (hn + g) * dh:].astype(BF16)], [BF16, BF16],
                  rope=[True, False], tables=tab_p, shift=ROPE_DIM // 2)
    qt = _heads_t(qk[:, :hn * dh], b, s, hn).reshape(b, g, rep, dh, s)
    k = _heads_first(qk[:, hn * dh:], b, s, g)
    sink_rows = jnp.broadcast_to(sinks.astype(F32).reshape(g, rep, 1, 1), (g, rep, 1, tq))
    ot = _attention(qt, k, _vt_blocks(v, b, s, g, tq), scale=dh ** -0.5, window=SWA_WINDOW,
                    sinks=sink_rows, tq=tq)
    return _tokens_major(ot, b, s)


def _xattn_kernel(q_ref, kv_ref, o_ref, *, scale):
    d = q_ref.shape[-1]
    hd = XATTN_HEAD_DIM
    for hh in range(XATTN_HEADS):
        q = q_ref[:, hh * hd:(hh + 1) * hd]
        k = kv_ref[:, hh * hd:(hh + 1) * hd]
        v = kv_ref[:, d + hh * hd:d + (hh + 1) * hd]
        s = lax.dot_general(q, k, (((1,), (1,)), ((), ())), preferred_element_type=F32) * scale
        m = jnp.max(s, axis=-1, keepdims=True)
        e = jnp.exp(s - m)
        p = e / jnp.sum(e, axis=-1, keepdims=True)
        o_ref[:, hh * hd:(hh + 1) * hd] = jnp.dot(p.astype(BF16), v, preferred_element_type=F32).astype(o_ref.dtype)


def _cross_attention(q, kv, b, s, tq=512):
    d = q.shape[-1]
    mlen = kv.shape[0] // b
    tq = min(tq, s)
    q3 = q.reshape(b, s, d)
    kv3 = kv.reshape(b, mlen, 2 * d)
    o = pl.pallas_call(
        functools.partial(_xattn_kernel, scale=XATTN_HEAD_DIM ** -0.5),
        out_shape=jax.ShapeDtypeStruct((b, s, d), BF16), grid=(b, s // tq),
        in_specs=[pl.BlockSpec((None, tq, d), lambda bi, i: (bi, i, 0)),
                  pl.BlockSpec((None, mlen, 2 * d), lambda bi, i: (bi, 0, 0))],
        out_specs=pl.BlockSpec((None, tq, d), lambda bi, i: (bi, i, 0)),
        compiler_params=_params("parallel", "parallel"), name="cross_attention",
    )(q3, kv3)
    return o.reshape(b * s, d)


def _swiglu_partial(xb, wg_ref, wu_ref, wo_ref):
    gte = jnp.dot(xb, wg_ref[...], preferred_element_type=F32)
    up = jnp.dot(xb, wu_ref[...], preferred_element_type=F32)
    act = gte * jax.nn.sigmoid(gte) * up
    return jnp.dot(act.astype(BF16), wo_ref[...], preferred_element_type=F32)


def _ffn_ln_kernel(x_ref, wg_ref, wu_ref, wo_ref, g_ref, b_ref, o_ref, acc_ref):
    f = pl.program_id(1)

    @pl.when(f == 0)
    def _():
        acc_ref[...] = jnp.zeros_like(acc_ref)

    acc_ref[...] += _swiglu_partial(x_ref[...].astype(BF16), wg_ref, wu_ref, wo_ref)

    @pl.when(f == pl.num_programs(1) - 1)
    def _():
        o_ref[...] = _ln(DEEPNORM_ALPHA * x_ref[...] + acc_ref[...], g_ref[...], b_ref[...])


def _ffn_ln(h, w_in, w_out, g, b, tm=1024, tf=512):
    m, d = h.shape
    dff = w_out.shape[0]
    tm = min(tm, m)
    nf = dff // tf
    row = pl.BlockSpec((tm, d), lambda i, f: (i, 0))
    vec = pl.BlockSpec((1, d), lambda i, f: (0, 0))
    return pl.pallas_call(
        _ffn_ln_kernel, out_shape=jax.ShapeDtypeStruct((m, d), F32), grid=(m // tm, nf),
        in_specs=[row,
                  pl.BlockSpec((d, tf), lambda i, f: (0, f)),
                  pl.BlockSpec((d, tf), lambda i, f: (0, nf + f)),
                  pl.BlockSpec((tf, d), lambda i, f: (f, 0)),
                  vec, vec],
        out_specs=row, scratch_shapes=[pltpu.VMEM((tm, d), F32)],
        compiler_params=_params("parallel", "arbitrary"), name="ffn_ln",
    )(h, w_in.astype(BF16), w_in.astype(BF16), w_out.astype(BF16), g.reshape(1, d), b.reshape(1, d))


def _router_kernel(x_ref, rh_ref, rl_ref, b_ref, comb_ref, sel_ref):
    x = x_ref[...]
    xh = x.astype(BF16)
    xl = (x - xh.astype(F32)).astype(BF16)
    rh, rl = rh_ref[...], rl_ref[...]
    lg = (jnp.dot(xh, rh, preferred_element_type=F32) + jnp.dot(xl, rh, preferred_element_type=F32)
          + jnp.dot(xh, rl, preferred_element_type=F32)) + b_ref[...]
    lane = lax.broadcasted_iota(jnp.int32, lg.shape, 1).astype(F32)
    lg = jnp.where(lane < N_EXPERTS, lg, NEG_INF)
    m1 = jnp.max(lg, axis=-1, keepdims=True)
    i1 = jnp.min(jnp.where(lg == m1, lane, float(LANES)), axis=-1, keepdims=True)
    lg2 = jnp.where(lane == i1, NEG_INF, lg)
    m2 = jnp.max(lg2, axis=-1, keepdims=True)
    i2 = jnp.min(jnp.where(lg2 == m2, lane, float(LANES)), axis=-1, keepdims=True)
    e2 = jnp.exp(m2 - m1)
    den = 1.0 + e2
    comb_ref[...] = jnp.where(lane == i1, 1.0 / den, 0.0) + jnp.where(lane == i2, e2 / den, 0.0)
    sel_ref[...] = jnp.where((lane == i1) | (lane == i2), 1.0, 0.0)


def _router(h, router, bias, tm=512):
    m, d = h.shape
    tm = min(tm, m)
    rp = jnp.pad(router.astype(F32), ((0, 0), (0, LANES - N_EXPERTS)))
    rh = rp.astype(BF16)
    rl = (rp - rh.astype(F32)).astype(BF16)
    bp = jnp.pad(bias.astype(F32), (0, LANES - N_EXPERTS)).reshape(1, LANES)
    row = pl.BlockSpec((tm, d), lambda i: (i, 0))
    wsp = pl.BlockSpec((d, LANES), lambda i: (0, 0))
    osp = pl.BlockSpec((tm, LANES), lambda i: (i, 0))
    return pl.pallas_call(
        _router_kernel, out_shape=[jax.ShapeDtypeStruct((m, LANES), F32)] * 2, grid=(m // tm,),
        in_specs=[row, wsp, wsp, pl.BlockSpec((1, LANES), lambda i: (0, 0))], out_specs=[osp, osp],
        compiler_params=_params("parallel"), name="moe_router",
    )(h, rh, rl, bp)


SCALAR_UNROLL = 8


def _slot_tokens_kernel(pos_ref, src_ref):
    t = pos_ref.shape[0] // 2

    def zero(i, c):
        src_ref[i] = 0
        return c

    lax.fori_loop(0, src_ref.shape[0], zero, 0, unroll=SCALAR_UNROLL)

    def body(i, c):
        src_ref[pos_ref[i]] = i
        src_ref[pos_ref[t + i]] = i
        return c

    lax.fori_loop(0, t, body, 0, unroll=SCALAR_UNROLL)


def _slot_tokens(pos2, tp):
    smem = pl.BlockSpec(memory_space=pltpu.SMEM)
    return pl.pallas_call(
        _slot_tokens_kernel, out_shape=jax.ShapeDtypeStruct((tp,), jnp.int32),
        in_specs=[smem], out_specs=smem, name="moe_slot_tokens",
    )(pos2)


def _moe_ffn_kernel(te_ref, tv_ref, x_ref, wg_ref, wu_ref, wo_ref, o_ref, acc_ref):
    i = pl.program_id(0)
    f = pl.program_id(1)

    @pl.when(f == 0)
    def _():
        acc_ref[...] = jnp.zeros_like(acc_ref)

    @pl.when(tv_ref[i] > 0)
    def _():
        acc_ref[...] += _swiglu_partial(x_ref[...], wg_ref, wu_ref, wo_ref)

    @pl.when(f == pl.num_programs(1) - 1)
    def _():
        o_ref[...] = acc_ref[...].astype(o_ref.dtype)


def _moe_ffn(xs, w_in, w_out, tile_expert, tile_valid, tm, tf=512):
    tp, d = xs.shape
    dff = w_out.shape[1]
    nf = dff // tf
    last = nf - 1

    def fblk(f, tv, i):
        return jnp.where(tv[i] > 0, f, last)

    grid_spec = pltpu.PrefetchScalarGridSpec(
        num_scalar_prefetch=2, grid=(tp // tm, nf),
        in_specs=[pl.BlockSpec((tm, d), lambda i, f, te, tv: (i, 0)),
                  pl.BlockSpec((None, d, tf), lambda i, f, te, tv: (te[i], 0, fblk(f, tv, i))),
                  pl.BlockSpec((None, d, tf), lambda i, f, te, tv: (te[i], 0, nf + fblk(f, tv, i))),
                  pl.BlockSpec((None, tf, d), lambda i, f, te, tv: (te[i], fblk(f, tv, i), 0))],
        out_specs=pl.BlockSpec((tm, d), lambda i, f, te, tv: (i, 0)),
        scratch_shapes=[pltpu.VMEM((tm, d), F32)])
    return pl.pallas_call(
        _moe_ffn_kernel, out_shape=jax.ShapeDtypeStruct((tp, d), BF16), grid_spec=grid_spec,
        compiler_params=_params("parallel", "arbitrary"), name="moe_ffn",
    )(tile_expert, tile_valid, xs, w_in, w_in, w_out)


def _moe_layer(h, router, bias, w_in, w_out, g, b, tm=512):
    t, d = h.shape
    tm = min(tm, t)
    comb, sel = _router(h, router, bias)
    comb, sel = comb[:, :N_EXPERTS], sel[:, :N_EXPERTS] > 0.5

    seli = sel.astype(jnp.int32)
    cnt = jnp.sum(seli, axis=0)
    ntile = (cnt + tm - 1) // tm
    tile_end = jnp.cumsum(ntile)
    row_start = (tile_end - ntile) * tm
    rank = jnp.cumsum(seli, axis=0) - seli
    n_tiles = (t * TOP_K) // tm + N_EXPERTS
    tp = n_tiles * tm
    pos = jnp.where(sel, row_start[None, :] + rank, tp)
    tiles = jnp.arange(n_tiles, dtype=jnp.int32)
    tile_valid = (tiles < tile_end[-1]).astype(jnp.int32)
    tile_expert = jnp.sum((tiles[:, None] >= tile_end[None, :]).astype(jnp.int32), axis=1)
    last_e = jnp.sum((tile_end[-1] - 1 >= tile_end).astype(jnp.int32))
    tile_expert = jnp.where(tile_valid > 0, tile_expert, last_e).astype(jnp.int32)
    p_lo = jnp.min(pos, axis=1)
    p_hi = jnp.max(jnp.where(sel, pos, -1), axis=1)
    src = _slot_tokens(jnp.concatenate([p_lo, p_hi]), tp)
    xs = jnp.take(h.astype(BF16), src, axis=0)

    ys = _moe_ffn(xs, w_in.astype(BF16), w_out.astype(BF16), tile_expert, tile_valid, tm)
    base = jnp.minimum((row_start[None, :] + rank[::tm]) // tm, n_tiles - 2).astype(jnp.int32).reshape(-1)
    pos_e = pos.T.reshape(N_EXPERTS, t, 1)
    gate_e = comb.T.reshape(N_EXPERTS, t, 1)
    return _combine_ln(h, ys, pos_e, gate_e, base, g, b, tm)


def kernel(x, mem, positions, l0_nsa_w_in, l0_nsa_cmp_pe, l0_nsa_cmp_w1, l0_nsa_cmp_w2, l0_nsa_w_out, l0_ln1_g, l0_ln1_b, l0_xq, l0_xkv, l0_xo, l0_ln2_g, l0_ln2_b, l0_ffn_w_in, l0_ffn_w_out, l0_ln3_g, l0_ln3_b, l1_mla_w_down, l1_mla_q_norm, l1_mla_kv_norm, l1_mla_w_uq, l1_mla_w_ukv, l1_mla_w_out, l1_ln1_g, l1_ln1_b, l1_xq, l1_xkv, l1_xo, l1_ln2_g, l1_ln2_b, l1_moe_router, l1_moe_bias, l1_moe_w_in, l1_moe_w_out, l1_ln3_g, l1_ln3_b, l2_moba_w_in, l2_moba_w_out, l2_ln1_g, l2_ln1_b, l2_xq, l2_xkv, l2_xo, l2_ln2_g, l2_ln2_b, l2_ffn_w_in, l2_ffn_w_out, l2_ln3_g, l2_ln3_b, l3_swa_w_in, l3_swa_sinks, l3_swa_w_out, l3_ln1_g, l3_ln1_b, l3_xq, l3_xkv, l3_xo, l3_ln2_g, l3_ln2_b, l3_moe_router, l3_moe_bias, l3_moe_w_in, l3_moe_w_out, l3_ln3_g, l3_ln3_b):
    b, s, d = x.shape
    t = b * s
    tab_p = _rope_tables(positions, ROPE_DIM, HEAD_DIM)
    tab_m = _rope_tables(positions, MLA_ROPE_DIM, MLA_ROPE_DIM)
    mem2 = mem.reshape(-1, d)

    mixers = [
        (lambda h: _nsa_mixer(h, b, s, tab_p, l0_nsa_w_in, l0_nsa_cmp_pe, l0_nsa_cmp_w1, l0_nsa_cmp_w2), l0_nsa_w_out),
        (lambda h: _mla_mixer(h, b, s, tab_m, l1_mla_w_down, l1_mla_q_norm, l1_mla_kv_norm, l1_mla_w_uq, l1_mla_w_ukv),
         l1_mla_w_out),
        (lambda h: _moba_mixer(h, b, s, tab_p, l2_moba_w_in), l2_moba_w_out),
        (lambda h: _swa_mixer(h, b, s, tab_p, l3_swa_w_in, l3_swa_sinks), l3_swa_w_out),
    ]
    ln1 = [(l0_ln1_g, l0_ln1_b), (l1_ln1_g, l1_ln1_b), (l2_ln1_g, l2_ln1_b), (l3_ln1_g, l3_ln1_b)]
    xattn = [(l0_xq, l0_xkv, l0_xo), (l1_xq, l1_xkv, l1_xo), (l2_xq, l2_xkv, l2_xo), (l3_xq, l3_xkv, l3_xo)]
    ln2 = [(l0_ln2_g, l0_ln2_b), (l1_ln2_g, l1_ln2_b), (l2_ln2_g, l2_ln2_b), (l3_ln2_g, l3_ln2_b)]
    ln3 = [(l0_ln3_g, l0_ln3_b), (l1_ln3_g, l1_ln3_b), (l2_ln3_g, l2_ln3_b), (l3_ln3_g, l3_ln3_b)]
    ffns = [
        lambda h, g, bb: _ffn_ln(h, l0_ffn_w_in, l0_ffn_w_out, g, bb),
        lambda h, g, bb: _moe_layer(h, l1_moe_router, l1_moe_bias, l1_moe_w_in, l1_moe_w_out, g, bb),
        lambda h, g, bb: _ffn_ln(h, l2_ffn_w_in, l2_ffn_w_out, g, bb),
        lambda h, g, bb: _moe_layer(h, l3_moe_router, l3_moe_bias, l3_moe_w_in, l3_moe_w_out, g, bb),
    ]

    h = x.reshape(t, d)
    for i in range(DEPTH):
        mixer, w_out = mixers[i]
        h = _mm_res_ln(mixer(h), w_out.astype(BF16), h, *ln1[i])
        w_q, w_kv, w_o = xattn[i]
        (q,) = _proj(h, [w_q.astype(BF16)], [BF16])
        (kv,) = _proj(mem2, [w_kv.astype(BF16)], [BF16])
        h = _mm_res_ln(_cross_attention(q, kv, b, s), w_o.astype(BF16), h, *ln2[i])
        h = ffns[i](h, *ln3[i])
    return h.reshape(b, s, d)
```

```python
import functools

import numpy as np
import jax
import jax.numpy as jnp
from jax import lax
from jax.experimental import pallas as pl
from jax.experimental.pallas import tpu as pltpu

BF16 = jnp.bfloat16
F32 = jnp.float32

D_MODEL = 1024
HEAD_DIM = 64
N_HEADS = D_MODEL // HEAD_DIM
ROPE_THETA = 500000.0
ROPE_DIM = HEAD_DIM // 4

NSA_KV_GROUPS = 4
NSA_CMP_BLOCK = 32
NSA_CMP_STRIDE = 16
NSA_SLC_BLOCK = 64
NSA_SLC_TOPN = 8
NSA_WINDOW = 512
NSA_FORCE_SCORE = 1e4

MLA_Q_RANK = D_MODEL // 4
MLA_KV_RANK = D_MODEL // 8
MLA_NOPE_DIM = 64
MLA_ROPE_DIM = 32
MLA_V_DIM = 64

MOBA_BLOCK = 256
MOBA_TOPK = 3

SWA_KV_HEADS = 2
SWA_WINDOW = 128

XATTN_HEADS = 4
XATTN_HEAD_DIM = D_MODEL // XATTN_HEADS

D_FF = (D_MODEL * 7) // 2
N_EXPERTS = 8
TOP_K = 2

DEPTH = 4
DEEPNORM_ALPHA = (2.0 * DEPTH) ** 0.25
LN_EPS = 1e-5
RMS_EPS = 1e-6

LANES = 128
SUBLANES = 8
NEG_INF = float("-inf")
M_INIT = -1e30
VMEM_LIMIT = 48 * 1024 * 1024


def _params(*sem):
    return pltpu.CompilerParams(dimension_semantics=sem, vmem_limit_bytes=VMEM_LIMIT)


def _ln(z, g, b):
    mu = jnp.mean(z, axis=-1, keepdims=True)
    zc = z - mu
    var = jnp.mean(zc * zc, axis=-1, keepdims=True)
    return zc * lax.rsqrt(var + LN_EPS) * g + b


def _div_pow2(x, d):
    assert d & (d - 1) == 0
    return x >> (d.bit_length() - 1)


def _is_pow2(v):
    m, _ = np.frexp(v)
    return m == 0.5


def _proj_kernel(*refs, n_out, rope, shift, rms, chunk):
    x_ref = refs[0]
    pos = 1
    if rms:
        g_ref = refs[pos]
        pos += 1
    w_refs = refs[pos:pos + n_out]
    pos += n_out
    if any(rope):
        c_ref, s1_ref, s2_ref = refs[pos:pos + 3]
        pos += 3
    o_refs = refs[pos:pos + n_out]

    x = x_ref[...]
    if rms:
        xf = x.astype(F32)
        xf = xf * lax.rsqrt(jnp.mean(xf * xf, axis=-1, keepdims=True) + RMS_EPS) * g_ref[...]
        xb = xf.astype(BF16)
    else:
        xb = x.astype(BF16)
    for i in range(n_out):
        n = w_refs[i].shape[1]
        for c0 in range(0, n, chunk):
            cw = min(chunk, n - c0)
            acc = jnp.dot(xb, w_refs[i][:, c0:c0 + cw], preferred_element_type=F32)
            if rope[i]:
                reps = cw // LANES
                c, s1, s2 = c_ref[...], s1_ref[...], s2_ref[...]
                if reps > 1:
                    c = jnp.concatenate([c] * reps, axis=1)
                    s1 = jnp.concatenate([s1] * reps, axis=1)
                    s2 = jnp.concatenate([s2] * reps, axis=1)
                acc = acc * c + pltpu.roll(acc, shift, 1) * s1 + pltpu.roll(acc, cw - shift, 1) * s2
            o_refs[i][:, c0:c0 + cw] = acc.astype(o_refs[i].dtype)


def _proj(x, ws, out_dtypes, *, rope=None, tables=None, shift=0, rms_g=None, tm=512):
    m, k = x.shape
    n_out = len(ws)
    rope = tuple(rope) if rope is not None else (False,) * n_out
    tm = min(tm, m)
    assert m % tm == 0
    in_specs = [pl.BlockSpec((tm, k), lambda i: (i, 0))]
    args = [x]
    if rms_g is not None:
        in_specs.append(pl.BlockSpec((1, k), lambda i: (0, 0)))
        args.append(rms_g.reshape(1, k).astype(F32))
    for w in ws:
        assert w.shape[1] % LANES == 0
        in_specs.append(pl.BlockSpec(w.shape, lambda i: (0, 0)))
        args.append(w)
    if any(rope):
        for t in tables:
            in_specs.append(pl.BlockSpec((tm, LANES), lambda i: (i, 0)))
            args.append(t)
    out_shape = [jax.ShapeDtypeStruct((m, w.shape[1]), dt) for w, dt in zip(ws, out_dtypes)]
    out_specs = [pl.BlockSpec((tm, w.shape[1]), lambda i: (i, 0)) for w in ws]
    kern = functools.partial(_proj_kernel, n_out=n_out, rope=rope, shift=shift,
                             rms=rms_g is not None, chunk=2 * LANES)
    return pl.pallas_call(kern, out_shape=out_shape, grid=(m // tm,), in_specs=in_specs,
                          out_specs=out_specs, compiler_params=_params("parallel"), name="proj")(*args)


def _rope_tables(positions, rot_dim, period):
    half = rot_dim // 2
    inv = ROPE_THETA ** (-jnp.arange(0, rot_dim, 2, dtype=F32) / rot_dim)
    ang = positions.astype(F32).reshape(-1)[:, None] * inv
    cos, sin = jnp.cos(ang), jnp.sin(ang)
    t = ang.shape[0]
    zero_h = jnp.zeros((t, half), F32)
    rest = period - rot_dim
    c = jnp.concatenate([cos, cos, jnp.ones((t, rest), F32)], axis=1)
    s1 = jnp.concatenate([zero_h, sin, jnp.zeros((t, rest), F32)], axis=1)
    s2 = jnp.concatenate([-sin, zero_h, jnp.zeros((t, rest), F32)], axis=1)
    reps = LANES // period
    return tuple(jnp.tile(a, (1, reps)) for a in (c, s1, s2))


def _mm_res_ln_kernel(a_ref, w_ref, h_ref, g_ref, b_ref, o_ref):
    y = jnp.dot(a_ref[...].astype(BF16), w_ref[...], preferred_element_type=F32)
    o_ref[...] = _ln(DEEPNORM_ALPHA * h_ref[...] + y, g_ref[...], b_ref[...])


def _mm_res_ln(a, w, h, g, b, tm=512):
    m, k = a.shape
    d = w.shape[1]
    tm = min(tm, m)
    return pl.pallas_call(
        _mm_res_ln_kernel,
        out_shape=jax.ShapeDtypeStruct((m, d), F32),
        grid=(m // tm,),
        in_specs=[pl.BlockSpec((tm, k), lambda i: (i, 0)),
                  pl.BlockSpec((k, d), lambda i: (0, 0)),
                  pl.BlockSpec((tm, d), lambda i: (i, 0)),
                  pl.BlockSpec((1, d), lambda i: (0, 0)),
                  pl.BlockSpec((1, d), lambda i: (0, 0))],
        out_specs=pl.BlockSpec((tm, d), lambda i: (i, 0)),
        compiler_params=_params("parallel"), name="mm_res_ln",
    )(a, w, h, g.reshape(1, d), b.reshape(1, d))


def _combine_ln_kernel(h_ref, a_ref, b2_ref, w_ref, g_ref, b_ref, o_ref):
    w = w_ref[...]
    y = w[:, 0:1] * a_ref[...].astype(F32) + w[:, 1:2] * b2_ref[...].astype(F32)
    o_ref[...] = _ln(DEEPNORM_ALPHA * h_ref[...] + y, g_ref[...], b_ref[...])


def _combine_ln(h, y2, w2, g, b, tm=512):
    m, d = h.shape
    tm = min(tm, m)
    nt = m // tm
    row = pl.BlockSpec((tm, d), lambda i: (i, 0))
    vec = pl.BlockSpec((1, d), lambda i: (0, 0))
    return pl.pallas_call(
        _combine_ln_kernel, out_shape=jax.ShapeDtypeStruct((m, d), F32), grid=(nt,),
        in_specs=[row, row, pl.BlockSpec((tm, d), lambda i: (nt + i, 0)),
                  pl.BlockSpec((tm, w2.shape[1]), lambda i: (i, 0)), vec, vec],
        out_specs=row, compiler_params=_params("parallel"), name="combine_ln",
    )(h, y2, y2, w2, g.reshape(1, d), b.reshape(1, d))


def _flash_t(qts, k_refs, vt_refs, q0, *, tq, tk, scale, window=None, m0s=None, bias_refs=None, blk=None):
    n = len(qts)
    dv = vt_refs[0].shape[1]
    fold = _is_pow2(scale)
    if fold:
        qts = [q * jnp.asarray(scale, q.dtype) for q in qts]
    qpos = q0 + lax.broadcasted_iota(jnp.int32, (1, tq), 1)
    carry = []
    for c in range(n):
        if m0s is None:
            carry += [jnp.full((1, tq), M_INIT, F32), jnp.zeros((1, tq), F32)]
        else:
            carry += [m0s[c], jnp.ones((1, tq), F32)]
        carry.append(jnp.zeros((dv, tq), F32))

    def make_body(masked):
        def body(j, carry):
            start = pl.multiple_of(j * tk, tk)
            if masked:
                delta = qpos - (start + lax.broadcasted_iota(jnp.int32, (tk, 1), 0))
                mask = delta >= 0
                if window is not None:
                    mask = mask & (delta < window)
            k_blocks, v_blocks, biases = {}, {}, {}
            out = []
            for c in range(n):
                m, l, acc = carry[3 * c:3 * c + 3]
                if id(k_refs[c]) not in k_blocks:
                    k_blocks[id(k_refs[c])] = k_refs[c][pl.ds(start, tk), :]
                    v_blocks[id(k_refs[c])] = vt_refs[c][j]
                s = jnp.dot(k_blocks[id(k_refs[c])], qts[c], preferred_element_type=F32)
                if not fold:
                    s = s * scale
                if bias_refs is not None:
                    br = bias_refs[c]
                    if id(br) not in biases:
                        per = tk // blk
                        bias = br[j, per - 1:per, :]
                        if per > 1:
                            sub = lax.broadcasted_iota(jnp.int32, (tk, tq), 0)
                            bias = jnp.broadcast_to(bias, (tk, tq))
                            for r in range(per - 2, -1, -1):
                                bias = jnp.where(sub < (r + 1) * blk, br[j, r:r + 1, :], bias)
                        biases[id(br)] = bias
                    s = s + biases[id(br)]
                if masked:
                    s = jnp.where(mask, s, NEG_INF)
                m_new = jnp.maximum(m, jnp.max(s, axis=0, keepdims=True))
                a = jnp.exp(m - m_new)
                p = jnp.exp(s - m_new)
                l_new = a * l + jnp.sum(p, axis=0, keepdims=True)
                acc_new = a * acc + jnp.dot(v_blocks[id(k_refs[c])], p.astype(BF16), preferred_element_type=F32)
                out += [m_new, l_new, acc_new]
            return tuple(out)
        return body

    diag = _div_pow2(q0, tk)
    hi = _div_pow2(q0 + tq - 1, tk)
    carry = tuple(carry)
    if window is None:
        lo_full = 0
    else:
        lo = _div_pow2(jnp.maximum(q0 - (window - 1), 0), tk)
        lo_full = jnp.minimum(_div_pow2(jnp.maximum(q0 + tq - window + tk - 1, 0), tk), diag)
        carry = lax.fori_loop(lo, lo_full, make_body(True), carry)
    carry = lax.fori_loop(lo_full, diag, make_body(False), carry)
    carry = lax.fori_loop(diag, hi + 1, make_body(True), carry)
    return [carry[3 * c + 2] / jnp.maximum(carry[3 * c + 1], 1e-30) for c in range(n)]


def _topk_rows(score, valid, k):
    row = lax.broadcasted_iota(jnp.int32, score.shape, 0)
    rank = jnp.zeros(score.shape, F32)
    for j in range(score.shape[0]):
        r = score[j:j + 1, :]
        beats = (r > score) | ((r == score) & (row > j))
        rank = rank + jnp.where(beats, 1.0, 0.0)
    return jnp.where(valid & (rank < k), 1.0, 0.0)


def _fill_bias(bias_ref, sel, per):
    bias = jnp.where(sel > 0.5, 0.0, NEG_INF)
    for jb in range(bias_ref.shape[0]):
        bias_ref[jb, 0:per, :] = bias[jb * per:(jb + 1) * per, :]


def _attn_kernel(*refs, hb, rep, tq, tk, scale, window, has_sink):
    if has_sink:
        qt_ref, k_ref, vt_ref, sink_ref, o_ref = refs
    else:
        qt_ref, k_ref, vt_ref, o_ref = refs
    q0 = pl.program_id(2) * tq
    heads = [(hh, r) for hh in range(hb) for r in range(rep)]
    k_views = [k_ref.at[hh] for hh in range(hb)]
    vt_views = [vt_ref.at[hh] for hh in range(hb)]
    outs = _flash_t([qt_ref[hh, r] for hh, r in heads], [k_views[hh] for hh, _ in heads],
                    [vt_views[hh] for hh, _ in heads], q0, tq=tq, tk=tk, scale=scale, window=window,
                    m0s=[sink_ref[hh, r] for hh, r in heads] if has_sink else None)
    for (hh, r), o in zip(heads, outs):
        o_ref[hh, r] = o.astype(o_ref.dtype)


def _attention(qt, k, vt, *, scale, window=None, sinks=None, tq=128, hb=1):
    b, g, rep, dk, s = qt.shape
    nblk, dv, tk = vt.shape[2:]
    tq = min(tq, s)
    assert g % hb == 0
    in_specs = [pl.BlockSpec((None, hb, rep, dk, tq), lambda bi, gi, i: (bi, gi, 0, 0, i)),
                pl.BlockSpec((None, hb, s, dk), lambda bi, gi, i: (bi, gi, 0, 0)),
                pl.BlockSpec((None, hb, nblk, dv, tk), lambda bi, gi, i: (bi, gi, 0, 0, 0))]
    args = [qt, k, vt]
    if sinks is not None:
        in_specs.append(pl.BlockSpec((hb, rep, 1, tq), lambda bi, gi, i: (gi, 0, 0, 0)))
        args.append(sinks)
    kern = functools.partial(_attn_kernel, hb=hb, rep=rep, tq=tq, tk=tk, scale=scale, window=window,
                             has_sink=sinks is not None)
    return pl.pallas_call(
        kern, out_shape=jax.ShapeDtypeStruct((b, g, rep, dv, s), BF16), grid=(b, g // hb, s // tq),
        in_specs=in_specs,
        out_specs=pl.BlockSpec((None, hb, rep, dv, tq), lambda bi, gi, i: (bi, gi, 0, 0, i)),
        compiler_params=_params("parallel", "parallel", "parallel"), name="attention",
    )(*args)


def _heads_first(t, b, s, h):
    return t.reshape(b, s, h, t.shape[-1] // h).transpose(0, 2, 1, 3)


def _heads_t(t, b, s, h):
    return t.reshape(b, s, h, t.shape[-1] // h).transpose(0, 2, 3, 1)


def _vt_blocks(t, b, s, h, tk):
    return t.reshape(b, s // tk, tk, h, t.shape[-1] // h).transpose(0, 3, 1, 4, 2)


def _tokens_major(ot, b, s):
    return ot.reshape(b, -1, s).transpose(0, 2, 1).reshape(b * s, -1)


def _gelu_tanh(x):
    return 0.5 * x * (1.0 + jnp.tanh(np.sqrt(2.0 / np.pi).astype(np.float32) * (x + 0.044715 * (x * x * x))))


def _nsa_compress_kernel(ak_ref, av_ref, pe_ref, w1_ref, w2_ref, kc_ref, vc_ref):
    half = ak_ref.shape[-1]
    ncp = ak_ref.shape[0]
    for j, (a_ref, o_ref) in enumerate(((ak_ref, kc_ref), (av_ref, vc_ref))):
        a = a_ref[...].astype(F32)
        lo = (a + pe_ref[j, 0:1, :]).astype(BF16)
        hi = (a + pe_ref[j, 1:2, :]).astype(BF16)
        h1 = jnp.dot(lo, w1_ref[j, :half, :], preferred_element_type=F32)
        h2 = jnp.dot(hi, w1_ref[j, half:, :], preferred_element_type=F32)
        h = _gelu_tanh(h1 + pltpu.roll(h2, ncp - 1, 0))
        o_ref[...] = jnp.dot(h.astype(BF16), w2_ref[j], preferred_element_type=F32).astype(o_ref.dtype)


def _nsa_compress(ak, av, pe2, w1, w2):
    b, g, ncp, wid = ak.shape
    dh = w2.shape[-1]
    blk = pl.BlockSpec((None, None, ncp, wid), lambda bi, gi: (bi, gi, 0, 0))
    full = lambda a: pl.BlockSpec(a.shape, lambda bi, gi: (0,) * a.ndim)
    out = pl.BlockSpec((None, None, ncp, dh), lambda bi, gi: (bi, gi, 0, 0))
    return pl.pallas_call(
        _nsa_compress_kernel,
        out_shape=[jax.ShapeDtypeStruct((b, g, ncp, dh), BF16)] * 2,
        grid=(b, g), in_specs=[blk, blk, full(pe2), full(w1), full(w2)], out_specs=[out, out],
        compiler_params=_params("parallel", "parallel"), name="nsa_compress",
    )(ak, av, pe2, w1, w2)


def _nsa_attn_kernel(qt_ref, kc_ref, vct_ref, ks_ref, vst_ref, kw_ref, vwt_ref, gl_ref, ovlt_ref, o_ref,
                     bias_ref, *, rep, tq, tk, nc, n_sel, scale):
    q0 = pl.program_id(2) * tq
    rows = rep * tq
    qts = [qt_ref[r] for r in range(rep)]
    qt = jnp.concatenate(qts, axis=1)
    row_pos = q0 + (lax.broadcasted_iota(jnp.int32, (1, rows), 1) & (tq - 1))

    ncp = kc_ref.shape[0]
    s = jnp.dot(kc_ref[...], qt * jnp.asarray(scale, qt.dtype), preferred_element_type=F32)
    cidx = lax.broadcasted_iota(jnp.int32, (ncp, 1), 0)
    cmask = (cidx * NSA_CMP_STRIDE + (NSA_CMP_BLOCK - 1) <= row_pos) & (cidx < nc)
    s = jnp.where(cmask, s, NEG_INF)
    m = jnp.max(s, axis=0, keepdims=True)
    m = jnp.where(m == NEG_INF, 0.0, m)
    e = jnp.exp(s - m)
    p = e / jnp.maximum(jnp.sum(e, axis=0, keepdims=True), 1e-30)
    o_cmp = jnp.dot(vct_ref[...], p.astype(BF16), preferred_element_type=F32)

    psum = p[:, 0:tq]
    for r in range(1, rep):
        psum = psum + p[:, r * tq:(r + 1) * tq]
    p_hi = psum.astype(BF16)
    p_lo = (psum - p_hi.astype(F32)).astype(BF16)
    ovlt = ovlt_ref[...]
    imp = (jnp.dot(ovlt, p_hi, preferred_element_type=F32)
           + jnp.dot(ovlt, p_lo, preferred_element_type=F32))
    nb = imp.shape[0]
    blk = lax.broadcasted_iota(jnp.int32, (nb, tq), 0)
    cur = _div_pow2(q0 + lax.broadcasted_iota(jnp.int32, (nb, tq), 1), NSA_SLC_BLOCK)
    forced = (blk == 0) | (blk == cur) | (blk == cur - 1)
    valid = blk <= cur
    score = jnp.where(valid, jnp.where(forced, NSA_FORCE_SCORE, imp), NEG_INF)
    _fill_bias(bias_ref, _topk_rows(score, valid, n_sel), tk // NSA_SLC_BLOCK)

    o_slc = _flash_t(qts, [ks_ref] * rep, [vst_ref] * rep, q0, tq=tq, tk=tk, scale=scale,
                     bias_refs=[bias_ref] * rep, blk=NSA_SLC_BLOCK)
    o_win = _flash_t(qts, [kw_ref] * rep, [vwt_ref] * rep, q0, tq=tq, tk=tk, scale=scale, window=NSA_WINDOW)

    for r in range(rep):
        gate = jax.nn.sigmoid(gl_ref[r])
        o = (gate[0:1, :] * o_cmp[:, r * tq:(r + 1) * tq] + gate[1:2, :] * o_slc[r] + gate[2:3, :] * o_win[r])
        o_ref[r] = o.astype(o_ref.dtype)


def _nsa_attention(qt, kc, vct, ks, vst, kw, vwt, gl, ovlt, *, nc, tq=128):
    b, g, rep, dh, s = qt.shape
    ncp = kc.shape[2]
    nblk, _, tk = vst.shape[2:]
    nb = s // NSA_SLC_BLOCK
    tq = min(tq, s)
    qspec = pl.BlockSpec((None, None, rep, dh, tq), lambda bi, gi, i: (bi, gi, 0, 0, i))
    kspec = pl.BlockSpec((None, None, s, dh), lambda bi, gi, i: (bi, gi, 0, 0))
    vspec = pl.BlockSpec((None, None, nblk, dh, tk), lambda bi, gi, i: (bi, gi, 0, 0, 0))
    kern = functools.partial(_nsa_attn_kernel, rep=rep, tq=tq, tk=tk, nc=nc,
                             n_sel=min(NSA_SLC_TOPN, nb), scale=dh ** -0.5)
    return pl.pallas_call(
        kern, out_shape=jax.ShapeDtypeStruct((b, g, rep, dh, s), BF16), grid=(b, g, s // tq),
        in_specs=[qspec,
                  pl.BlockSpec((None, None, ncp, dh), lambda bi, gi, i: (bi, gi, 0, 0)),
                  pl.BlockSpec((None, None, dh, ncp), lambda bi, gi, i: (bi, gi, 0, 0)),
                  kspec, vspec, kspec, vspec,
                  pl.BlockSpec((None, None, rep, gl.shape[3], tq), lambda bi, gi, i: (bi, gi, 0, 0, i)),
                  pl.BlockSpec(ovlt.shape, lambda bi, gi, i: (0, 0))],
        out_specs=qspec,
        scratch_shapes=[pltpu.VMEM((nblk, SUBLANES, tq), F32)],
        compiler_params=_params("parallel", "parallel", "parallel"), name="nsa_attention",
    )(qt, kc, vct, ks, vst, kw, vwt, gl, ovlt)


def _nsa_mixer(h, b, s, tab_p, w_in, cmp_pe, cmp_w1, cmp_w2):
    hn, g, dh = N_HEADS, NSA_KV_GROUPS, HEAD_DIM
    rep = hn // g
    gd = g * dh
    q_end = hn * dh
    tk = min(128, s)
    kv = lambda j: w_in[:, q_end + j * gd:q_end + (j + 1) * gd]
    w_rope = jnp.concatenate([w_in[:, :q_end], kv(0), kv(2), kv(4)], axis=1).astype(BF16)
    w_plain = jnp.concatenate([kv(1), kv(3), kv(5)], axis=1).astype(BF16)
    n_gate = 3 * hn
    w_gate = jnp.pad(w_in[:, q_end + 6 * gd:], ((0, 0), (0, LANES - n_gate))).astype(BF16)
    roped, plain, gates = _proj(h, [w_rope, w_plain, w_gate], [BF16, BF16, F32],
                                rope=[True, False, False], tables=tab_p, shift=ROPE_DIM // 2)
    qt = _heads_t(roped[:, :q_end], b, s, hn).reshape(b, g, rep, dh, s)
    k_c, k_s, k_w = [_heads_first(roped[:, q_end + j * gd:q_end + (j + 1) * gd], b, s, g) for j in range(3)]
    v_c = _heads_first(plain[:, :gd], b, s, g)
    vst, vwt = [_vt_blocks(plain[:, j * gd:(j + 1) * gd], b, s, g, tk) for j in (1, 2)]
    gl = gates[:, :n_gate].reshape(b, s, 3, g, rep).transpose(0, 3, 4, 2, 1)

    nc = (s - NSA_CMP_BLOCK) // NSA_CMP_STRIDE + 1
    ncp = s // NSA_CMP_STRIDE
    half = NSA_CMP_STRIDE * dh
    ak = k_c.reshape(b, g, ncp, half)
    av = v_c.reshape(b, g, ncp, half)
    pe2 = cmp_pe.reshape(2, 2, half).astype(F32)
    kc, vc = _nsa_compress(ak, av, pe2, cmp_w1.astype(BF16), cmp_w2.astype(BF16))
    vct = vc.transpose(0, 1, 3, 2)

    nb = s // NSA_SLC_BLOCK
    c_start = np.arange(ncp) * NSA_CMP_STRIDE
    b_start = np.arange(nb) * NSA_SLC_BLOCK
    overlap = np.clip(np.minimum(c_start[:, None] + NSA_CMP_BLOCK, b_start[None, :] + NSA_SLC_BLOCK)
                      - np.maximum(c_start[:, None], b_start[None, :]), 0, None) / NSA_CMP_BLOCK
    overlap[nc:] = 0.0
    ovlt = jnp.asarray(overlap.T, BF16)
    ot = _nsa_attention(qt, kc, vct, k_s, vst, k_w, vwt, gl, ovlt, nc=nc)
    return _tokens_major(ot, b, s)


def _mla_mixer(h, b, s, tab_m, w_down, q_norm, kv_norm, w_uq, w_ukv):
    hn = N_HEADS
    qr_, kvr = MLA_Q_RANK, MLA_KV_RANK
    w_kr = jnp.pad(w_down[:, qr_ + kvr:], ((0, 0), (0, LANES - MLA_ROPE_DIM)))
    cq, ckv, kr = _proj(h, [w_down[:, :qr_].astype(BF16), w_down[:, qr_:qr_ + kvr].astype(BF16),
                            w_kr.astype(BF16)], [F32, F32, BF16],
                        rope=[False, False, True], tables=tab_m, shift=MLA_ROPE_DIM // 2)
    w_uq3 = w_uq.reshape(qr_, hn, MLA_NOPE_DIM + MLA_ROPE_DIM)
    w_qn = w_uq3[:, :, :MLA_NOPE_DIM].reshape(qr_, hn * MLA_NOPE_DIM).astype(BF16)
    w_qr = w_uq3[:, :, MLA_NOPE_DIM:].reshape(qr_, hn * MLA_ROPE_DIM).astype(BF16)
    qn, qr = _proj(cq, [w_qn, w_qr], [BF16, BF16], rope=[False, True], tables=tab_m,
                   shift=MLA_ROPE_DIM // 2, rms_g=q_norm)
    w_ukv3 = w_ukv.reshape(kvr, hn, MLA_NOPE_DIM + MLA_V_DIM)
    w_kn = w_ukv3[:, :, :MLA_NOPE_DIM].reshape(kvr, hn * MLA_NOPE_DIM).astype(BF16)
    w_v = w_ukv3[:, :, MLA_NOPE_DIM:].reshape(kvr, hn * MLA_V_DIM).astype(BF16)
    kn, v = _proj(ckv, [w_kn, w_v], [BF16, BF16], rms_g=kv_norm)

    tk = min(256, s)
    dpad = LANES - MLA_NOPE_DIM - MLA_ROPE_DIM
    qt = jnp.concatenate([_heads_t(qn, b, s, hn), _heads_t(qr, b, s, hn),
                          jnp.zeros((b, hn, dpad, s), BF16)], axis=2)
    k_rope = jnp.broadcast_to(kr[:, :MLA_ROPE_DIM].reshape(b, 1, s, MLA_ROPE_DIM), (b, hn, s, MLA_ROPE_DIM))
    k = jnp.concatenate([_heads_first(kn, b, s, hn), k_rope, jnp.zeros((b, hn, s, dpad), BF16)], axis=-1)
    scale = (MLA_NOPE_DIM + MLA_ROPE_DIM) ** -0.5
    ot = _attention(qt[:, :, None], k, _vt_blocks(v, b, s, hn, tk), scale=scale, tq=256, hb=4)
    return _tokens_major(ot, b, s)


def _moba_kernel(qt_ref, k_ref, vt_ref, o_ref, bias_ref, *, hb, tq, n_sel, scale):
    q0 = pl.program_id(2) * tq
    s_len, dh = k_ref.shape[1:]
    nb = s_len // MOBA_BLOCK
    nbp = 2 * SUBLANES
    sub = lax.broadcasted_iota(jnp.int32, (nbp, dh), 0)
    blk = lax.broadcasted_iota(jnp.int32, (nb, tq), 0)
    own = _div_pow2(q0 + lax.broadcasted_iota(jnp.int32, (nb, tq), 1), MOBA_BLOCK)
    valid = blk < own
    qts = [qt_ref[hh] for hh in range(hb)]
    for hh in range(hb):
        k_mean = jnp.zeros((nbp, dh), F32)
        for j in range(nb):
            kj = k_ref[hh, j * MOBA_BLOCK:(j + 1) * MOBA_BLOCK, :].astype(F32)
            k_mean = jnp.where(sub == j, jnp.sum(kj, axis=0, keepdims=True) * (1.0 / MOBA_BLOCK), k_mean)
        km_hi = k_mean.astype(BF16)
        km_lo = (k_mean - km_hi.astype(F32)).astype(BF16)
        gate = (jnp.dot(km_hi, qts[hh], preferred_element_type=F32)
                + jnp.dot(km_lo, qts[hh], preferred_element_type=F32))[:nb]
        score = jnp.where(valid, gate, NEG_INF)
        sel = jnp.maximum(_topk_rows(score, valid, n_sel), jnp.where(blk == own, 1.0, 0.0))
        _fill_bias(bias_ref.at[hh], sel, 1)
    outs = _flash_t(qts, [k_ref.at[hh] for hh in range(hb)], [vt_ref.at[hh] for hh in range(hb)], q0,
                    tq=tq, tk=MOBA_BLOCK, scale=scale,
                    bias_refs=[bias_ref.at[hh] for hh in range(hb)], blk=MOBA_BLOCK)
    for hh in range(hb):
        o_ref[hh] = outs[hh].astype(o_ref.dtype)


def _moba_attention(qt, k, vt, tq=256, hb=4):
    b, hn, dh, s = qt.shape
    nb = s // MOBA_BLOCK
    n_sel = max(1, min(MOBA_TOPK, nb - 1))
    tq = min(tq, s)
    assert nb <= 2 * SUBLANES and hn % hb == 0
    qspec = pl.BlockSpec((None, hb, dh, tq), lambda bi, hi, i: (bi, hi, 0, i))
    kern = functools.partial(_moba_kernel, hb=hb, tq=tq, n_sel=n_sel, scale=dh ** -0.5)
    return pl.pallas_call(
        kern, out_shape=jax.ShapeDtypeStruct((b, hn, dh, s), BF16), grid=(b, hn // hb, s // tq),
        in_specs=[qspec,
                  pl.BlockSpec((None, hb, s, dh), lambda bi, hi, i: (bi, hi, 0, 0)),
                  pl.BlockSpec((None, hb, nb, dh, MOBA_BLOCK), lambda bi, hi, i: (bi, hi, 0, 0, 0))],
        out_specs=qspec,
        scratch_shapes=[pltpu.VMEM((hb, nb, SUBLANES, tq), F32)],
        compiler_params=_params("parallel", "parallel", "parallel"), name="moba_attention",
    )(qt, k, vt)


def _moba_mixer(h, b, s, tab_p, w_in):
    hn, dh = N_HEADS, HEAD_DIM
    assert s % MOBA_BLOCK == 0
    qk, v = _proj(h, [w_in[:, :2 * hn * dh].astype(BF16), w_in[:, 2 * hn * dh:].astype(BF16)], [BF16, BF16],
                  rope=[True, False], tables=tab_p, shift=ROPE_DIM // 2)
    qt = _heads_t(qk[:, :hn * dh], b, s, hn)
    k = _heads_first(qk[:, hn * dh:], b, s, hn)
    ot = _moba_attention(qt, k, _vt_blocks(v, b, s, hn, MOBA_BLOCK))
    return _tokens_major(ot, b, s)


def _swa_mixer(h, b, s, tab_p, w_in, sinks):
    hn, g, dh = N_HEADS, SWA_KV_HEADS, HEAD_DIM
    rep = hn // g
    tq = min(128, s)
    qk, v = _proj(h, [w_in[:, :(hn + g) * dh].astype(BF16), w_in[:, (hn + g) * dh:].astype(BF16)], [BF16, BF16],
                  rope=[True, False], tables=tab_p, shift=ROPE_DIM // 2)
    qt = _heads_t(qk[:, :hn * dh], b, s, hn).reshape(b, g, rep, dh, s)
    k = _heads_first(qk[:, hn * dh:], b, s, g)
    sink_rows = jnp.broadcast_to(sinks.astype(F32).reshape(g, rep, 1, 1), (g, rep, 1, tq))
    ot = _attention(qt, k, _vt_blocks(v, b, s, g, tq), scale=dh ** -0.5, window=SWA_WINDOW,
                    sinks=sink_rows, tq=tq)
    return _tokens_major(ot, b, s)


def _xattn_kernel(q_ref, kv_ref, o_ref, *, scale):
    d = q_ref.shape[-1]
    hd = XATTN_HEAD_DIM
    for hh in range(XATTN_HEADS):
        q = q_ref[:, hh * hd:(hh + 1) * hd]
        k = kv_ref[:, hh * hd:(hh + 1) * hd]
        v = kv_ref[:, d + hh * hd:d + (hh + 1) * hd]
        s = lax.dot_general(q, k, (((1,), (1,)), ((), ())), preferred_element_type=F32) * scale
        m = jnp.max(s, axis=-1, keepdims=True)
        e = jnp.exp(s - m)
        p = e / jnp.sum(e, axis=-1, keepdims=True)
        o_ref[:, hh * hd:(hh + 1) * hd] = jnp.dot(p.astype(BF16), v, preferred_element_type=F32).astype(o_ref.dtype)


def _cross_attention(q, kv, b, s, tq=512):
    d = q.shape[-1]
    mlen = kv.shape[0] // b
    tq = min(tq, s)
    q3 = q.reshape(b, s, d)
    kv3 = kv.reshape(b, mlen, 2 * d)
    o = pl.pallas_call(
        functools.partial(_xattn_kernel, scale=XATTN_HEAD_DIM ** -0.5),
        out_shape=jax.ShapeDtypeStruct((b, s, d), BF16), grid=(b, s // tq),
        in_specs=[pl.BlockSpec((None, tq, d), lambda bi, i: (bi, i, 0)),
                  pl.BlockSpec((None, mlen, 2 * d), lambda bi, i: (bi, 0, 0))],
        out_specs=pl.BlockSpec((None, tq, d), lambda bi, i: (bi, i, 0)),
        compiler_params=_params("parallel", "parallel"), name="cross_attention",
    )(q3, kv3)
    return o.reshape(b * s, d)


def _swiglu_partial(xb, wg_ref, wu_ref, wo_ref):
    gte = jnp.dot(xb, wg_ref[...], preferred_element_type=F32)
    up = jnp.dot(xb, wu_ref[...], preferred_element_type=F32)
    act = gte * jax.nn.sigmoid(gte) * up
    return jnp.dot(act.astype(BF16), wo_ref[...], preferred_element_type=F32)


def _ffn_ln_kernel(x_ref, wg_ref, wu_ref, wo_ref, g_ref, b_ref, o_ref, acc_ref):
    f = pl.program_id(1)

    @pl.when(f == 0)
    def _():
        acc_ref[...] = jnp.zeros_like(acc_ref)

    acc_ref[...] += _swiglu_partial(x_ref[...].astype(BF16), wg_ref, wu_ref, wo_ref)

    @pl.when(f == pl.num_programs(1) - 1)
    def _():
        o_ref[...] = _ln(DEEPNORM_ALPHA * x_ref[...] + acc_ref[...], g_ref[...], b_ref[...])


def _ffn_ln(h, w_in, w_out, g, b, tm=1024, tf=512):
    m, d = h.shape
    dff = w_out.shape[0]
    tm = min(tm, m)
    nf = dff // tf
    row = pl.BlockSpec((tm, d), lambda i, f: (i, 0))
    vec = pl.BlockSpec((1, d), lambda i, f: (0, 0))
    return pl.pallas_call(
        _ffn_ln_kernel, out_shape=jax.ShapeDtypeStruct((m, d), F32), grid=(m // tm, nf),
        in_specs=[row,
                  pl.BlockSpec((d, tf), lambda i, f: (0, f)),
                  pl.BlockSpec((d, tf), lambda i, f: (0, nf + f)),
                  pl.BlockSpec((tf, d), lambda i, f: (f, 0)),
                  vec, vec],
        out_specs=row, scratch_shapes=[pltpu.VMEM((tm, d), F32)],
        compiler_params=_params("parallel", "arbitrary"), name="ffn_ln",
    )(h, w_in.astype(BF16), w_in.astype(BF16), w_out.astype(BF16), g.reshape(1, d), b.reshape(1, d))


def _router_kernel(x_ref, rh_ref, rl_ref, b_ref, comb_ref, sel_ref):
    x = x_ref[...]
    xh = x.astype(BF16)
    xl = (x - xh.astype(F32)).astype(BF16)
    rh, rl = rh_ref[...], rl_ref[...]
    lg = (jnp.dot(xh, rh, preferred_element_type=F32) + jnp.dot(xl, rh, preferred_element_type=F32)
          + jnp.dot(xh, rl, preferred_element_type=F32)) + b_ref[...]
    lane = lax.broadcasted_iota(jnp.int32, lg.shape, 1).astype(F32)
    lg = jnp.where(lane < N_EXPERTS, lg, NEG_INF)
    m1 = jnp.max(lg, axis=-1, keepdims=True)
    i1 = jnp.min(jnp.where(lg == m1, lane, float(LANES)), axis=-1, keepdims=True)
    lg2 = jnp.where(lane == i1, NEG_INF, lg)
    m2 = jnp.max(lg2, axis=-1, keepdims=True)
    i2 = jnp.min(jnp.where(lg2 == m2, lane, float(LANES)), axis=-1, keepdims=True)
    e2 = jnp.exp(m2 - m1)
    den = 1.0 + e2
    comb_ref[...] = jnp.where(lane == i1, 1.0 / den, 0.0) + jnp.where(lane == i2, e2 / den, 0.0)
    sel_ref[...] = jnp.where((lane == i1) | (lane == i2), 1.0, 0.0)


def _router(h, router, bias, tm=512):
    m, d = h.shape
    tm = min(tm, m)
    rp = jnp.pad(router.astype(F32), ((0, 0), (0, LANES - N_EXPERTS)))
    rh = rp.astype(BF16)
    rl = (rp - rh.astype(F32)).astype(BF16)
    bp = jnp.pad(bias.astype(F32), (0, LANES - N_EXPERTS)).reshape(1, LANES)
    row = pl.BlockSpec((tm, d), lambda i: (i, 0))
    wsp = pl.BlockSpec((d, LANES), lambda i: (0, 0))
    osp = pl.BlockSpec((tm, LANES), lambda i: (i, 0))
    return pl.pallas_call(
        _router_kernel, out_shape=[jax.ShapeDtypeStruct((m, LANES), F32)] * 2, grid=(m // tm,),
        in_specs=[row, wsp, wsp, pl.BlockSpec((1, LANES), lambda i: (0, 0))], out_specs=[osp, osp],
        compiler_params=_params("parallel"), name="moe_router",
    )(h, rh, rl, bp)


SCALAR_UNROLL = 8


def _slot_tokens_kernel(pos_ref, src_ref):
    t = pos_ref.shape[0] // 2

    def zero(i, c):
        src_ref[i] = 0
        return c

    lax.fori_loop(0, src_ref.shape[0], zero, 0, unroll=SCALAR_UNROLL)

    def body(i, c):
        src_ref[pos_ref[i]] = i
        src_ref[pos_ref[t + i]] = i
        return c

    lax.fori_loop(0, t, body, 0, unroll=SCALAR_UNROLL)


def _slot_tokens(pos2, tp):
    smem = pl.BlockSpec(memory_space=pltpu.SMEM)
    return pl.pallas_call(
        _slot_tokens_kernel, out_shape=jax.ShapeDtypeStruct((tp,), jnp.int32),
        in_specs=[smem], out_specs=smem, name="moe_slot_tokens",
    )(pos2)


def _moe_ffn_kernel(te_ref, tv_ref, x_ref, wg_ref, wu_ref, wo_ref, o_ref, acc_ref):
    i = pl.program_id(0)
    f = pl.program_id(1)

    @pl.when(f == 0)
    def _():
        acc_ref[...] = jnp.zeros_like(acc_ref)

    @pl.when(tv_ref[i] > 0)
    def _():
        acc_ref[...] += _swiglu_partial(x_ref[...].astype(BF16), wg_ref, wu_ref, wo_ref)

    @pl.when(f == pl.num_programs(1) - 1)
    def _():
        o_ref[...] = acc_ref[...].astype(o_ref.dtype)


def _moe_ffn(xs, w_in, w_out, tile_expert, tile_valid, tm, tf=512):
    tp, d = xs.shape
    dff = w_out.shape[1]
    nf = dff // tf
    last = nf - 1

    def fblk(f, tv, i):
        return jnp.where(tv[i] > 0, f, last)

    grid_spec = pltpu.PrefetchScalarGridSpec(
        num_scalar_prefetch=2, grid=(tp // tm, nf),
        in_specs=[pl.BlockSpec((tm, d), lambda i, f, te, tv: (i, 0)),
                  pl.BlockSpec((None, d, tf), lambda i, f, te, tv: (te[i], 0, fblk(f, tv, i))),
                  pl.BlockSpec((None, d, tf), lambda i, f, te, tv: (te[i], 0, nf + fblk(f, tv, i))),
                  pl.BlockSpec((None, tf, d), lambda i, f, te, tv: (te[i], fblk(f, tv, i), 0))],
        out_specs=pl.BlockSpec((tm, d), lambda i, f, te, tv: (i, 0)),
        scratch_shapes=[pltpu.VMEM((tm, d), F32)])
    return pl.pallas_call(
        _moe_ffn_kernel, out_shape=jax.ShapeDtypeStruct((tp, d), BF16), grid_spec=grid_spec,
        compiler_params=_params("parallel", "arbitrary"), name="moe_ffn",
    )(tile_expert, tile_valid, xs, w_in, w_in, w_out)


def _moe_layer(h, router, bias, w_in, w_out, g, b, tm=512):
    t, d = h.shape
    tm = min(tm, t)
    comb, sel = _router(h, router, bias)
    comb, sel = comb[:, :N_EXPERTS], sel[:, :N_EXPERTS] > 0.5

    seli = sel.astype(jnp.int32)
    cnt = jnp.sum(seli, axis=0)
    ntile = (cnt + tm - 1) // tm
    tile_end = jnp.cumsum(ntile)
    row_start = (tile_end - ntile) * tm
    rank = jnp.cumsum(seli, axis=0) - seli
    n_tiles = (t * TOP_K) // tm + N_EXPERTS
    tp = n_tiles * tm
    pos = jnp.where(sel, row_start[None, :] + rank, tp)
    tiles = jnp.arange(n_tiles, dtype=jnp.int32)
    tile_valid = (tiles < tile_end[-1]).astype(jnp.int32)
    tile_expert = jnp.sum((tiles[:, None] >= tile_end[None, :]).astype(jnp.int32), axis=1)
    last_e = jnp.sum((tile_end[-1] - 1 >= tile_end).astype(jnp.int32))
    tile_expert = jnp.where(tile_valid > 0, tile_expert, last_e).astype(jnp.int32)
    p_lo = jnp.min(pos, axis=1)
    p_hi = jnp.max(jnp.where(sel, pos, -1), axis=1)
    w2 = jnp.stack([jnp.sum(jnp.where(pos == p_lo[:, None], comb, 0.0), axis=1),
                    jnp.sum(jnp.where(pos == p_hi[:, None], comb, 0.0), axis=1)], axis=1)
    pos2 = jnp.concatenate([p_lo, p_hi])
    xs = jnp.take(h, _slot_tokens(pos2, tp), axis=0)
    ys = _moe_ffn(xs, w_in.astype(BF16), w_out.astype(BF16), tile_expert, tile_valid, tm)
    return _combine_ln(h, jnp.take(ys, pos2, axis=0), w2, g, b)


def kernel(x, mem, positions, l0_nsa_w_in, l0_nsa_cmp_pe, l0_nsa_cmp_w1, l0_nsa_cmp_w2, l0_nsa_w_out, l0_ln1_g, l0_ln1_b, l0_xq, l0_xkv, l0_xo, l0_ln2_g, l0_ln2_b, l0_ffn_w_in, l0_ffn_w_out, l0_ln3_g, l0_ln3_b, l1_mla_w_down, l1_mla_q_norm, l1_mla_kv_norm, l1_mla_w_uq, l1_mla_w_ukv, l1_mla_w_out, l1_ln1_g, l1_ln1_b, l1_xq, l1_xkv, l1_xo, l1_ln2_g, l1_ln2_b, l1_moe_router, l1_moe_bias, l1_moe_w_in, l1_moe_w_out, l1_ln3_g, l1_ln3_b, l2_moba_w_in, l2_moba_w_out, l2_ln1_g, l2_ln1_b, l2_xq, l2_xkv, l2_xo, l2_ln2_g, l2_ln2_b, l2_ffn_w_in, l2_ffn_w_out, l2_ln3_g, l2_ln3_b, l3_swa_w_in, l3_swa_sinks, l3_swa_w_out, l3_ln1_g, l3_ln1_b, l3_xq, l3_xkv, l3_xo, l3_ln2_g, l3_ln2_b, l3_moe_router, l3_moe_bias, l3_moe_w_in, l3_moe_w_out, l3_ln3_g, l3_ln3_b):
    b, s, d = x.shape
    t = b * s
    tab_p = _rope_tables(positions, ROPE_DIM, HEAD_DIM)
    tab_m = _rope_tables(positions, MLA_ROPE_DIM, MLA_ROPE_DIM)
    mem2 = mem.reshape(-1, d)

    mixers = [
        (lambda h: _nsa_mixer(h, b, s, tab_p, l0_nsa_w_in, l0_nsa_cmp_pe, l0_nsa_cmp_w1, l0_nsa_cmp_w2), l0_nsa_w_out),
        (lambda h: _mla_mixer(h, b, s, tab_m, l1_mla_w_down, l1_mla_q_norm, l1_mla_kv_norm, l1_mla_w_uq, l1_mla_w_ukv),
         l1_mla_w_out),
        (lambda h: _moba_mixer(h, b, s, tab_p, l2_moba_w_in), l2_moba_w_out),
        (lambda h: _swa_mixer(h, b, s, tab_p, l3_swa_w_in, l3_swa_sinks), l3_swa_w_out),
    ]
    ln1 = [(l0_ln1_g, l0_ln1_b), (l1_ln1_g, l1_ln1_b), (l2_ln1_g, l2_ln1_b), (l3_ln1_g, l3_ln1_b)]
    xattn = [(l0_xq, l0_xkv, l0_xo), (l1_xq, l1_xkv, l1_xo), (l2_xq, l2_xkv, l2_xo), (l3_xq, l3_xkv, l3_xo)]
    ln2 = [(l0_ln2_g, l0_ln2_b), (l1_ln2_g, l1_ln2_b), (l2_ln2_g, l2_ln2_b), (l3_ln2_g, l3_ln2_b)]
    ln3 = [(l0_ln3_g, l0_ln3_b), (l1_ln3_g, l1_ln3_b), (l2_ln3_g, l2_ln3_b), (l3_ln3_g, l3_ln3_b)]
    ffns = [
        lambda h, g, bb: _ffn_ln(h, l0_ffn_w_in, l0_ffn_w_out, g, bb),
        lambda h, g, bb: _moe_layer(h, l1_moe_router, l1_moe_bias, l1_moe_w_in, l1_moe_w_out, g, bb),
        lambda h, g, bb: _ffn_ln(h, l2_ffn_w_in, l2_ffn_w_out, g, bb),
        lambda h, g, bb: _moe_layer(h, l3_moe_router, l3_moe_bias, l3_moe_w_in, l3_moe_w_out, g, bb),
    ]

    h = x.reshape(t, d)
    for i in range(DEPTH):
        mixer, w_out = mixers[i]
        h = _mm_res_ln(mixer(h), w_out.astype(BF16), h, *ln1[i])
        w_q, w_kv, w_o = xattn[i]
        (q,) = _proj(h, [w_q.astype(BF16)], [BF16])
        (kv,) = _proj(mem2, [w_kv.astype(BF16)], [BF16])
        h = _mm_res_ln(_cross_attention(q, kv, b, s), w_o.astype(BF16), h, *ln2[i])
        h = ffns[i](h, *ln3[i])
    return h.reshape(b, s, d)
```

```python
import functools

import numpy as np
import jax
import jax.numpy as jnp
from jax import lax
from jax.experimental import pallas as pl
from jax.experimental.pallas import tpu as pltpu

BF16 = jnp.bfloat16
F32 = jnp.float32

D_MODEL = 1024
HEAD_DIM = 64
N_HEADS = D_MODEL // HEAD_DIM
ROPE_THETA = 500000.0
ROPE_DIM = HEAD_DIM // 4

NSA_KV_GROUPS = 4
NSA_CMP_BLOCK = 32
NSA_CMP_STRIDE = 16
NSA_SLC_BLOCK = 64
NSA_SLC_TOPN = 8
NSA_WINDOW = 512
NSA_FORCE_SCORE = 1e4

MLA_Q_RANK = D_MODEL // 4
MLA_KV_RANK = D_MODEL // 8
MLA_NOPE_DIM = 64
MLA_ROPE_DIM = 32
MLA_V_DIM = 64

MOBA_BLOCK = 256
MOBA_TOPK = 3

SWA_KV_HEADS = 2
SWA_WINDOW = 128

XATTN_HEADS = 4
XATTN_HEAD_DIM = D_MODEL // XATTN_HEADS

D_FF = (D_MODEL * 7) // 2
N_EXPERTS = 8
TOP_K = 2

DEPTH = 4
DEEPNORM_ALPHA = (2.0 * DEPTH) ** 0.25
LN_EPS = 1e-5
RMS_EPS = 1e-6

LANES = 128
SUBLANES = 8
NEG_INF = float("-inf")
M_INIT = -1e30
VMEM_LIMIT = 48 * 1024 * 1024


def _params(*sem):
    return pltpu.CompilerParams(dimension_semantics=sem, vmem_limit_bytes=VMEM_LIMIT)


def _ln(z, g, b):
    mu = jnp.mean(z, axis=-1, keepdims=True)
    zc = z - mu
    var = jnp.mean(zc * zc, axis=-1, keepdims=True)
    return zc * lax.rsqrt(var + LN_EPS) * g + b


def _div_pow2(x, d):
    assert d & (d - 1) == 0
    return x >> (d.bit_length() - 1)


def _is_pow2(v):
    m, _ = np.frexp(v)
    return m == 0.5


def _proj_kernel(*refs, n_out, rope, shift, rms, chunk):
    x_ref = refs[0]
    pos = 1
    if rms:
        g_ref = refs[pos]
        pos += 1
    w_refs = refs[pos:pos + n_out]
    pos += n_out
    if any(rope):
        c_ref, s1_ref, s2_ref = refs[pos:pos + 3]
        pos += 3
    o_refs = refs[pos:pos + n_out]

    x = x_ref[...]
    if rms:
        xf = x.astype(F32)
        xf = xf * lax.rsqrt(jnp.mean(xf * xf, axis=-1, keepdims=True) + RMS_EPS) * g_ref[...]
        xb = xf.astype(BF16)
    else:
        xb = x.astype(BF16)
    for i in range(n_out):
        n = w_refs[i].shape[1]
        for c0 in range(0, n, chunk):
            cw = min(chunk, n - c0)
            acc = jnp.dot(xb, w_refs[i][:, c0:c0 + cw], preferred_element_type=F32)
            if rope[i]:
                reps = cw // LANES
                c, s1, s2 = c_ref[...], s1_ref[...], s2_ref[...]
                if reps > 1:
                    c = jnp.concatenate([c] * reps, axis=1)
                    s1 = jnp.concatenate([s1] * reps, axis=1)
                    s2 = jnp.concatenate([s2] * reps, axis=1)
                acc = acc * c + pltpu.roll(acc, shift, 1) * s1 + pltpu.roll(acc, cw - shift, 1) * s2
            o_refs[i][:, c0:c0 + cw] = acc.astype(o_refs[i].dtype)


def _proj(x, ws, out_dtypes, *, rope=None, tables=None, shift=0, rms_g=None, tm=512):
    m, k = x.shape
    n_out = len(ws)
    rope = tuple(rope) if rope is not None else (False,) * n_out
    tm = min(tm, m)
    assert m % tm == 0
    in_specs = [pl.BlockSpec((tm, k), lambda i: (i, 0))]
    args = [x]
    if rms_g is not None:
        in_specs.append(pl.BlockSpec((1, k), lambda i: (0, 0)))
        args.append(rms_g.reshape(1, k).astype(F32))
    for w in ws:
        assert w.shape[1] % LANES == 0
        in_specs.append(pl.BlockSpec(w.shape, lambda i: (0, 0)))
        args.append(w)
    if any(rope):
        for t in tables:
            in_specs.append(pl.BlockSpec((tm, LANES), lambda i: (i, 0)))
            args.append(t)
    out_shape = [jax.ShapeDtypeStruct((m, w.shape[1]), dt) for w, dt in zip(ws, out_dtypes)]
    out_specs = [pl.BlockSpec((tm, w.shape[1]), lambda i: (i, 0)) for w in ws]
    kern = functools.partial(_proj_kernel, n_out=n_out, rope=rope, shift=shift,
                             rms=rms_g is not None, chunk=2 * LANES)
    return pl.pallas_call(kern, out_shape=out_shape, grid=(m // tm,), in_specs=in_specs,
                          out_specs=out_specs, compiler_params=_params("parallel"), name="proj")(*args)


def _rope_tables(positions, rot_dim, period):
    half = rot_dim // 2
    inv = ROPE_THETA ** (-jnp.arange(0, rot_dim, 2, dtype=F32) / rot_dim)
    ang = positions.astype(F32).reshape(-1)[:, None] * inv
    cos, sin = jnp.cos(ang), jnp.sin(ang)
    t = ang.shape[0]
    zero_h = jnp.zeros((t, half), F32)
    rest = period - rot_dim
    c = jnp.concatenate([cos, cos, jnp.ones((t, rest), F32)], axis=1)
    s1 = jnp.concatenate([zero_h, sin, jnp.zeros((t, rest), F32)], axis=1)
    s2 = jnp.concatenate([-sin, zero_h, jnp.zeros((t, rest), F32)], axis=1)
    reps = LANES // period
    return tuple(jnp.tile(a, (1, reps)) for a in (c, s1, s2))


def _mm_res_ln_kernel(a_ref, w_ref, h_ref, g_ref, b_ref, o_ref):
    y = jnp.dot(a_ref[...].astype(BF16), w_ref[...], preferred_element_type=F32)
    o_ref[...] = _ln(DEEPNORM_ALPHA * h_ref[...] + y, g_ref[...], b_ref[...])


def _mm_res_ln(a, w, h, g, b, tm=512):
    m, k = a.shape
    d = w.shape[1]
    tm = min(tm, m)
    return pl.pallas_call(
        _mm_res_ln_kernel,
        out_shape=jax.ShapeDtypeStruct((m, d), F32),
        grid=(m // tm,),
        in_specs=[pl.BlockSpec((tm, k), lambda i: (i, 0)),
                  pl.BlockSpec((k, d), lambda i: (0, 0)),
                  pl.BlockSpec((tm, d), lambda i: (i, 0)),
                  pl.BlockSpec((1, d), lambda i: (0, 0)),
                  pl.BlockSpec((1, d), lambda i: (0, 0))],
        out_specs=pl.BlockSpec((tm, d), lambda i: (i, 0)),
        compiler_params=_params("parallel"), name="mm_res_ln",
    )(a, w, h, g.reshape(1, d), b.reshape(1, d))


def _combine_ln_kernel(h_ref, a_ref, b2_ref, w_ref, g_ref, b_ref, o_ref):
    w = w_ref[...]
    y = w[:, 0:1] * a_ref[...].astype(F32) + w[:, 1:2] * b2_ref[...].astype(F32)
    o_ref[...] = _ln(DEEPNORM_ALPHA * h_ref[...] + y, g_ref[...], b_ref[...])


def _combine_ln(h, y2, w2, g, b, tm=512):
    m, d = h.shape
    tm = min(tm, m)
    nt = m // tm
    row = pl.BlockSpec((tm, d), lambda i: (i, 0))
    vec = pl.BlockSpec((1, d), lambda i: (0, 0))
    return pl.pallas_call(
        _combine_ln_kernel, out_shape=jax.ShapeDtypeStruct((m, d), F32), grid=(nt,),
        in_specs=[row, row, pl.BlockSpec((tm, d), lambda i: (nt + i, 0)),
                  pl.BlockSpec((tm, w2.shape[1]), lambda i: (i, 0)), vec, vec],
        out_specs=row, compiler_params=_params("parallel"), name="combine_ln",
    )(h, y2, y2, w2, g.reshape(1, d), b.reshape(1, d))


def _flash_t(qts, k_refs, vt_refs, q0, *, tq, tk, scale, window=None, m0s=None, bias_refs=None, blk=None):
    n = len(qts)
    dv = vt_refs[0].shape[1]
    fold = _is_pow2(scale)
    if fold:
        qts = [q * jnp.asarray(scale, q.dtype) for q in qts]
    qpos = q0 + lax.broadcasted_iota(jnp.int32, (1, tq), 1)
    carry = []
    for c in range(n):
        if m0s is None:
            carry += [jnp.full((1, tq), M_INIT, F32), jnp.zeros((1, tq), F32)]
        else:
            carry += [m0s[c], jnp.ones((1, tq), F32)]
        carry.append(jnp.zeros((dv, tq), F32))

    def make_body(masked):
        def body(j, carry):
            start = pl.multiple_of(j * tk, tk)
            if masked:
                delta = qpos - (start + lax.broadcasted_iota(jnp.int32, (tk, 1), 0))
                mask = delta >= 0
                if window is not None:
                    mask = mask & (delta < window)
            k_blocks, v_blocks, biases = {}, {}, {}
            out = []
            for c in range(n):
                m, l, acc = carry[3 * c:3 * c + 3]
                if id(k_refs[c]) not in k_blocks:
                    k_blocks[id(k_refs[c])] = k_refs[c][pl.ds(start, tk), :]
                    v_blocks[id(k_refs[c])] = vt_refs[c][j]
                s = jnp.dot(k_blocks[id(k_refs[c])], qts[c], preferred_element_type=F32)
                if not fold:
                    s = s * scale
                if bias_refs is not None:
                    br = bias_refs[c]
                    if id(br) not in biases:
                        per = tk // blk
                        bias = br[j, per - 1:per, :]
                        if per > 1:
                            sub = lax.broadcasted_iota(jnp.int32, (tk, tq), 0)
                            bias = jnp.broadcast_to(bias, (tk, tq))
                            for r in range(per - 2, -1, -1):
                                bias = jnp.where(sub < (r + 1) * blk, br[j, r:r + 1, :], bias)
                        biases[id(br)] = bias
                    s = s + biases[id(br)]
                if masked:
                    s = jnp.where(mask, s, NEG_INF)
                m_new = jnp.maximum(m, jnp.max(s, axis=0, keepdims=True))
                a = jnp.exp(m - m_new)
                p = jnp.exp(s - m_new)
                l_new = a * l + jnp.sum(p, axis=0, keepdims=True)
                acc_new = a * acc + jnp.dot(v_blocks[id(k_refs[c])], p.astype(BF16), preferred_element_type=F32)
                out += [m_new, l_new, acc_new]
            return tuple(out)
        return body

    diag = _div_pow2(q0, tk)
    hi = _div_pow2(q0 + tq - 1, tk)
    carry = tuple(carry)
    if window is None:
        lo_full = 0
    else:
        lo = _div_pow2(jnp.maximum(q0 - (window - 1), 0), tk)
        lo_full = jnp.minimum(_div_pow2(jnp.maximum(q0 + tq - window + tk - 1, 0), tk), diag)
        carry = lax.fori_loop(lo, lo_full, make_body(True), carry)
    carry = lax.fori_loop(lo_full, diag, make_body(False), carry)
    carry = lax.fori_loop(diag, hi + 1, make_body(True), carry)
    return [carry[3 * c + 2] / jnp.maximum(carry[3 * c + 1], 1e-30) for c in range(n)]


def _topk_rows(score, valid, k):
    row = lax.broadcasted_iota(jnp.int32, score.shape, 0)
    rank = jnp.zeros(score.shape, F32)
    for j in range(score.shape[0]):
        r = score[j:j + 1, :]
        beats = (r > score) | ((r == score) & (row > j))
        rank = rank + jnp.where(beats, 1.0, 0.0)
    return jnp.where(valid & (rank < k), 1.0, 0.0)


def _fill_bias(bias_ref, sel, per):
    bias = jnp.where(sel > 0.5, 0.0, NEG_INF)
    for jb in range(bias_ref.shape[0]):
        bias_ref[jb, 0:per, :] = bias[jb * per:(jb + 1) * per, :]


def _attn_kernel(*refs, hb, rep, tq, tk, scale, window, has_sink):
    if has_sink:
        qt_ref, k_ref, vt_ref, sink_ref, o_ref = refs
    else:
        qt_ref, k_ref, vt_ref, o_ref = refs
    q0 = pl.program_id(2) * tq
    heads = [(hh, r) for hh in range(hb) for r in range(rep)]
    k_views = [k_ref.at[hh] for hh in range(hb)]
    vt_views = [vt_ref.at[hh] for hh in range(hb)]
    outs = _flash_t([qt_ref[hh, r] for hh, r in heads], [k_views[hh] for hh, _ in heads],
                    [vt_views[hh] for hh, _ in heads], q0, tq=tq, tk=tk, scale=scale, window=window,
                    m0s=[sink_ref[hh, r] for hh, r in heads] if has_sink else None)
    for (hh, r), o in zip(heads, outs):
        o_ref[hh, r] = o.astype(o_ref.dtype)


def _attention(qt, k, vt, *, scale, window=None, sinks=None, tq=128, hb=1):
    b, g, rep, dk, s = qt.shape
    nblk, dv, tk = vt.shape[2:]
    tq = min(tq, s)
    assert g % hb == 0
    in_specs = [pl.BlockSpec((None, hb, rep, dk, tq), lambda bi, gi, i: (bi, gi, 0, 0, i)),
                pl.BlockSpec((None, hb, s, dk), lambda bi, gi, i: (bi, gi, 0, 0)),
                pl.BlockSpec((None, hb, nblk, dv, tk), lambda bi, gi, i: (bi, gi, 0, 0, 0))]
    args = [qt, k, vt]
    if sinks is not None:
        in_specs.append(pl.BlockSpec((hb, rep, 1, tq), lambda bi, gi, i: (gi, 0, 0, 0)))
        args.append(sinks)
    kern = functools.partial(_attn_kernel, hb=hb, rep=rep, tq=tq, tk=tk, scale=scale, window=window,
                             has_sink=sinks is not None)
    return pl.pallas_call(
        kern, out_shape=jax.ShapeDtypeStruct((b, g, rep, dv, s), BF16), grid=(b, g // hb, s // tq),
        in_specs=in_specs,
        out_specs=pl.BlockSpec((None, hb, rep, dv, tq), lambda bi, gi, i: (bi, gi, 0, 0, i)),
        compiler_params=_params("parallel", "parallel", "parallel"), name="attention",
    )(*args)


def _heads_first(t, b, s, h):
    return t.reshape(b, s, h, t.shape[-1] // h).transpose(0, 2, 1, 3)


def _heads_t(t, b, s, h):
    return t.reshape(b, s, h, t.shape[-1] // h).transpose(0, 2, 3, 1)


def _vt_blocks(t, b, s, h, tk):
    return t.reshape(b, s // tk, tk, h, t.shape[-1] // h).transpose(0, 3, 1, 4, 2)


def _tokens_major(ot, b, s):
    return ot.reshape(b, -1, s).transpose(0, 2, 1).reshape(b * s, -1)


def _gelu_tanh(x):
    return 0.5 * x * (1.0 + jnp.tanh(np.sqrt(2.0 / np.pi).astype(np.float32) * (x + 0.044715 * (x * x * x))))


def _nsa_compress_kernel(ak_ref, av_ref, pe_ref, w1_ref, w2_ref, kc_ref, vc_ref):
    half = ak_ref.shape[-1]
    ncp = ak_ref.shape[0]
    for j, (a_ref, o_ref) in enumerate(((ak_ref, kc_ref), (av_ref, vc_ref))):
        a = a_ref[...].astype(F32)
        lo = (a + pe_ref[j, 0:1, :]).astype(BF16)
        hi = (a + pe_ref[j, 1:2, :]).astype(BF16)
        h1 = jnp.dot(lo, w1_ref[j, :half, :], preferred_element_type=F32)
        h2 = jnp.dot(hi, w1_ref[j, half:, :], preferred_element_type=F32)
        h = _gelu_tanh(h1 + pltpu.roll(h2, ncp - 1, 0))
        o_ref[...] = jnp.dot(h.astype(BF16), w2_ref[j], preferred_element_type=F32).astype(o_ref.dtype)


def _nsa_compress(ak, av, pe2, w1, w2):
    b, g, ncp, wid = ak.shape
    dh = w2.shape[-1]
    blk = pl.BlockSpec((None, None, ncp, wid), lambda bi, gi: (bi, gi, 0, 0))
    full = lambda a: pl.BlockSpec(a.shape, lambda bi, gi: (0,) * a.ndim)
    out = pl.BlockSpec((None, None, ncp, dh), lambda bi, gi: (bi, gi, 0, 0))
    return pl.pallas_call(
        _nsa_compress_kernel,
        out_shape=[jax.ShapeDtypeStruct((b, g, ncp, dh), BF16)] * 2,
        grid=(b, g), in_specs=[blk, blk, full(pe2), full(w1), full(w2)], out_specs=[out, out],
        compiler_params=_params("parallel", "parallel"), name="nsa_compress",
    )(ak, av, pe2, w1, w2)


def _nsa_attn_kernel(qt_ref, kc_ref, vct_ref, ks_ref, vst_ref, kw_ref, vwt_ref, gl_ref, ovlt_ref, o_ref,
                     bias_ref, *, rep, tq, tk, nc, n_sel, scale):
    q0 = pl.program_id(2) * tq
    rows = rep * tq
    qts = [qt_ref[r] for r in range(rep)]
    qt = jnp.concatenate(qts, axis=1)
    row_pos = q0 + (lax.broadcasted_iota(jnp.int32, (1, rows), 1) & (tq - 1))

    ncp = kc_ref.shape[0]
    s = jnp.dot(kc_ref[...], qt * jnp.asarray(scale, qt.dtype), preferred_element_type=F32)
    cidx = lax.broadcasted_iota(jnp.int32, (ncp, 1), 0)
    cmask = (cidx * NSA_CMP_STRIDE + (NSA_CMP_BLOCK - 1) <= row_pos) & (cidx < nc)
    s = jnp.where(cmask, s, NEG_INF)
    m = jnp.max(s, axis=0, keepdims=True)
    m = jnp.where(m == NEG_INF, 0.0, m)
    e = jnp.exp(s - m)
    p = e / jnp.maximum(jnp.sum(e, axis=0, keepdims=True), 1e-30)
    o_cmp = jnp.dot(vct_ref[...], p.astype(BF16), preferred_element_type=F32)

    psum = p[:, 0:tq]
    for r in range(1, rep):
        psum = psum + p[:, r * tq:(r + 1) * tq]
    p_hi = psum.astype(BF16)
    p_lo = (psum - p_hi.astype(F32)).astype(BF16)
    ovlt = ovlt_ref[...]
    imp = (jnp.dot(ovlt, p_hi, preferred_element_type=F32)
           + jnp.dot(ovlt, p_lo, preferred_element_type=F32))
    nb = imp.shape[0]
    blk = lax.broadcasted_iota(jnp.int32, (nb, tq), 0)
    cur = _div_pow2(q0 + lax.broadcasted_iota(jnp.int32, (nb, tq), 1), NSA_SLC_BLOCK)
    forced = (blk == 0) | (blk == cur) | (blk == cur - 1)
    valid = blk <= cur
    score = jnp.where(valid, jnp.where(forced, NSA_FORCE_SCORE, imp), NEG_INF)
    _fill_bias(bias_ref, _topk_rows(score, valid, n_sel), tk // NSA_SLC_BLOCK)

    o_slc = _flash_t(qts, [ks_ref] * rep, [vst_ref] * rep, q0, tq=tq, tk=tk, scale=scale,
                     bias_refs=[bias_ref] * rep, blk=NSA_SLC_BLOCK)
    o_win = _flash_t(qts, [kw_ref] * rep, [vwt_ref] * rep, q0, tq=tq, tk=tk, scale=scale, window=NSA_WINDOW)

    for r in range(rep):
        gate = jax.nn.sigmoid(gl_ref[r])
        o = (gate[0:1, :] * o_cmp[:, r * tq:(r + 1) * tq] + gate[1:2, :] * o_slc[r] + gate[2:3, :] * o_win[r])
        o_ref[r] = o.astype(o_ref.dtype)


def _nsa_attention(qt, kc, vct, ks, vst, kw, vwt, gl, ovlt, *, nc, tq=128):
    b, g, rep, dh, s = qt.shape
    ncp = kc.shape[2]
    nblk, _, tk = vst.shape[2:]
    nb = s // NSA_SLC_BLOCK
    tq = min(tq, s)
    qspec = pl.BlockSpec((None, None, rep, dh, tq), lambda bi, gi, i: (bi, gi, 0, 0, i))
    kspec = pl.BlockSpec((None, None, s, dh), lambda bi, gi, i: (bi, gi, 0, 0))
    vspec = pl.BlockSpec((None, None, nblk, dh, tk), lambda bi, gi, i: (bi, gi, 0, 0, 0))
    kern = functools.partial(_nsa_attn_kernel, rep=rep, tq=tq, tk=tk, nc=nc,
                             n_sel=min(NSA_SLC_TOPN, nb), scale=dh ** -0.5)
    return pl.pallas_call(
        kern, out_shape=jax.ShapeDtypeStruct((b, g, rep, dh, s), BF16), grid=(b, g, s // tq),
        in_specs=[qspec,
                  pl.BlockSpec((None, None, ncp, dh), lambda bi, gi, i: (bi, gi, 0, 0)),
                  pl.BlockSpec((None, None, dh, ncp), lambda bi, gi, i: (bi, gi, 0, 0)),
                  kspec, vspec, kspec, vspec,
                  pl.BlockSpec((None, None, rep, gl.shape[3], tq), lambda bi, gi, i: (bi, gi, 0, 0, i)),
                  pl.BlockSpec(ovlt.shape, lambda bi, gi, i: (0, 0))],
        out_specs=qspec,
        scratch_shapes=[pltpu.VMEM((nblk, SUBLANES, tq), F32)],
        compiler_params=_params("parallel", "parallel", "parallel"), name="nsa_attention",
    )(qt, kc, vct, ks, vst, kw, vwt, gl, ovlt)


def _nsa_mixer(h, b, s, tab_p, w_in, cmp_pe, cmp_w1, cmp_w2):
    hn, g, dh = N_HEADS, NSA_KV_GROUPS, HEAD_DIM
    rep = hn // g
    gd = g * dh
    q_end = hn * dh
    tk = min(128, s)
    kv = lambda j: w_in[:, q_end + j * gd:q_end + (j + 1) * gd]
    w_rope = jnp.concatenate([w_in[:, :q_end], kv(0), kv(2), kv(4)], axis=1).astype(BF16)
    w_plain = jnp.concatenate([kv(1), kv(3), kv(5)], axis=1).astype(BF16)
    n_gate = 3 * hn
    w_gate = jnp.pad(w_in[:, q_end + 6 * gd:], ((0, 0), (0, LANES - n_gate))).astype(BF16)
    roped, plain, gates = _proj(h, [w_rope, w_plain, w_gate], [BF16, BF16, F32],
                                rope=[True, False, False], tables=tab_p, shift=ROPE_DIM // 2)
    qt = _heads_t(roped[:, :q_end], b, s, hn).reshape(b, g, rep, dh, s)
    k_c, k_s, k_w = [_heads_first(roped[:, q_end + j * gd:q_end + (j + 1) * gd], b, s, g) for j in range(3)]
    v_c = _heads_first(plain[:, :gd], b, s, g)
    vst, vwt = [_vt_blocks(plain[:, j * gd:(j + 1) * gd], b, s, g, tk) for j in (1, 2)]
    gl = gates[:, :n_gate].reshape(b, s, 3, g, rep).transpose(0, 3, 4, 2, 1)

    nc = (s - NSA_CMP_BLOCK) // NSA_CMP_STRIDE + 1
    ncp = s // NSA_CMP_STRIDE
    half = NSA_CMP_STRIDE * dh
    ak = k_c.reshape(b, g, ncp, half)
    av = v_c.reshape(b, g, ncp, half)
    pe2 = cmp_pe.reshape(2, 2, half).astype(F32)
    kc, vc = _nsa_compress(ak, av, pe2, cmp_w1.astype(BF16), cmp_w2.astype(BF16))
    vct = vc.transpose(0, 1, 3, 2)

    nb = s // NSA_SLC_BLOCK
    c_start = np.arange(ncp) * NSA_CMP_STRIDE
    b_start = np.arange(nb) * NSA_SLC_BLOCK
    overlap = np.clip(np.minimum(c_start[:, None] + NSA_CMP_BLOCK, b_start[None, :] + NSA_SLC_BLOCK)
                      - np.maximum(c_start[:, None], b_start[None, :]), 0, None) / NSA_CMP_BLOCK
    overlap[nc:] = 0.0
    ovlt = jnp.asarray(overlap.T, BF16)
    ot = _nsa_attention(qt, kc, vct, k_s, vst, k_w, vwt, gl, ovlt, nc=nc)
    return _tokens_major(ot, b, s)


def _mla_mixer(h, b, s, tab_m, w_down, q_norm, kv_norm, w_uq, w_ukv):
    hn = N_HEADS
    qr_, kvr = MLA_Q_RANK, MLA_KV_RANK
    w_kr = jnp.pad(w_down[:, qr_ + kvr:], ((0, 0), (0, LANES - MLA_ROPE_DIM)))
    cq, ckv, kr = _proj(h, [w_down[:, :qr_].astype(BF16), w_down[:, qr_:qr_ + kvr].astype(BF16),
                            w_kr.astype(BF16)], [F32, F32, BF16],
                        rope=[False, False, True], tables=tab_m, shift=MLA_ROPE_DIM // 2)
    w_uq3 = w_uq.reshape(qr_, hn, MLA_NOPE_DIM + MLA_ROPE_DIM)
    w_qn = w_uq3[:, :, :MLA_NOPE_DIM].reshape(qr_, hn * MLA_NOPE_DIM).astype(BF16)
    w_qr = w_uq3[:, :, MLA_NOPE_DIM:].reshape(qr_, hn * MLA_ROPE_DIM).astype(BF16)
    qn, qr = _proj(cq, [w_qn, w_qr], [BF16, BF16], rope=[False, True], tables=tab_m,
                   shift=MLA_ROPE_DIM // 2, rms_g=q_norm)
    w_ukv3 = w_ukv.reshape(kvr, hn, MLA_NOPE_DIM + MLA_V_DIM)
    w_kn = w_ukv3[:, :, :MLA_NOPE_DIM].reshape(kvr, hn * MLA_NOPE_DIM).astype(BF16)
    w_v = w_ukv3[:, :, MLA_NOPE_DIM:].reshape(kvr, hn * MLA_V_DIM).astype(BF16)
    kn, v = _proj(ckv, [w_kn, w_v], [BF16, BF16], rms_g=kv_norm)

    tk = min(256, s)
    dpad = LANES - MLA_NOPE_DIM - MLA_ROPE_DIM
    qt = jnp.concatenate([_heads_t(qn, b, s, hn), _heads_t(qr, b, s, hn),
                          jnp.zeros((b, hn, dpad, s), BF16)], axis=2)
    k_rope = jnp.broadcast_to(kr[:, :MLA_ROPE_DIM].reshape(b, 1, s, MLA_ROPE_DIM), (b, hn, s, MLA_ROPE_DIM))
    k = jnp.concatenate([_heads_first(kn, b, s, hn), k_rope, jnp.zeros((b, hn, s, dpad), BF16)], axis=-1)
    scale = (MLA_NOPE_DIM + MLA_ROPE_DIM) ** -0.5
    ot = _attention(qt[:, :, None], k, _vt_blocks(v, b, s, hn, tk), scale=scale, tq=256, hb=4)
    return _tokens_major(ot, b, s)


def _moba_kernel(qt_ref, k_ref, vt_ref, o_ref, bias_ref, *, hb, tq, n_sel, scale):
    q0 = pl.program_id(2) * tq
    s_len, dh = k_ref.shape[1:]
    nb = s_len // MOBA_BLOCK
    nbp = 2 * SUBLANES
    sub = lax.broadcasted_iota(jnp.int32, (nbp, dh), 0)
    blk = lax.broadcasted_iota(jnp.int32, (nb, tq), 0)
    own = _div_pow2(q0 + lax.broadcasted_iota(jnp.int32, (nb, tq), 1), MOBA_BLOCK)
    valid = blk < own
    qts = [qt_ref[hh] for hh in range(hb)]
    for hh in range(hb):
        k_mean = jnp.zeros((nbp, dh), F32)
        for j in range(nb):
            kj = k_ref[hh, j * MOBA_BLOCK:(j + 1) * MOBA_BLOCK, :].astype(F32)
            k_mean = jnp.where(sub == j, jnp.sum(kj, axis=0, keepdims=True) * (1.0 / MOBA_BLOCK), k_mean)
        km_hi = k_mean.astype(BF16)
        km_lo = (k_mean - km_hi.astype(F32)).astype(BF16)
        gate = (jnp.dot(km_hi, qts[hh], preferred_element_type=F32)
                + jnp.dot(km_lo, qts[hh], preferred_element_type=F32))[:nb]
        score = jnp.where(valid, gate, NEG_INF)
        sel = jnp.maximum(_topk_rows(score, valid, n_sel), jnp.where(blk == own, 1.0, 0.0))
        _fill_bias(bias_ref.at[hh], sel, 1)
    outs = _flash_t(qts, [k_ref.at[hh] for hh in range(hb)], [vt_ref.at[hh] for hh in range(hb)], q0,
                    tq=tq, tk=MOBA_BLOCK, scale=scale,
                    bias_refs=[bias_ref.at[hh] for hh in range(hb)], blk=MOBA_BLOCK)
    for hh in range(hb):
        o_ref[hh] = outs[hh].astype(o_ref.dtype)


def _moba_attention(qt, k, vt, tq=256, hb=4):
    b, hn, dh, s = qt.shape
    nb = s // MOBA_BLOCK
    n_sel = max(1, min(MOBA_TOPK, nb - 1))
    tq = min(tq, s)
    assert nb <= 2 * SUBLANES and hn % hb == 0
    qspec = pl.BlockSpec((None, hb, dh, tq), lambda bi, hi, i: (bi, hi, 0, i))
    kern = functools.partial(_moba_kernel, hb=hb, tq=tq, n_sel=n_sel, scale=dh ** -0.5)
    return pl.pallas_call(
        kern, out_shape=jax.ShapeDtypeStruct((b, hn, dh, s), BF16), grid=(b, hn // hb, s // tq),
        in_specs=[qspec,
                  pl.BlockSpec((None, hb, s, dh), lambda bi, hi, i: (bi, hi, 0, 0)),
                  pl.BlockSpec((None, hb, nb, dh, MOBA_BLOCK), lambda bi, hi, i: (bi, hi, 0, 0, 0))],
        out_specs=qspec,
        scratch_shapes=[pltpu.VMEM((hb, nb, SUBLANES, tq), F32)],
        compiler_params=_params("parallel", "parallel", "parallel"), name="moba_attention",
    )(qt, k, vt)


def _moba_mixer(h, b, s, tab_p, w_in):
    hn, dh = N_HEADS, HEAD_DIM
    assert s % MOBA_BLOCK == 0
    qk, v = _proj(h, [w_in[:, :2 * hn * dh].astype(BF16), w_in[:, 2 * hn * dh:].astype(BF16)], [BF16, BF16],
                  rope=[True, False], tables=tab_p, shift=ROPE_DIM // 2)
    qt = _heads_t(qk[:, :hn * dh], b, s, hn)
    k = _heads_first(qk[:, hn * dh:], b, s, hn)
    ot = _moba_attention(qt, k, _vt_blocks(v, b, s, hn, MOBA_BLOCK))
    return _tokens_major(ot, b, s)


def _swa_mixer(h, b, s, tab_p, w_in, sinks):
    hn, g, dh = N_HEADS, SWA_KV_HEADS, HEAD_DIM
    rep = hn // g
    tq = min(128, s)
    qk, v = _proj(h, [w_in[:, :(hn + g) * dh].astype(BF16), w_in[:, (hn + g) * dh:].astype(BF16)], [BF16, BF16],
                  rope=[True, False], tables=tab_p, shift=ROPE_DIM // 2)
    qt = _heads_t(qk[:, :hn * dh], b, s, hn).reshape(b, g, rep, dh, s)
    k = _heads_first(qk[:, hn * dh:], b, s, g)
    sink_rows = jnp.broadcast_to(sinks.astype(F32).reshape(g, rep, 1, 1), (g, rep, 1, tq))
    ot = _attention(qt, k, _vt_blocks(v, b, s, g, tq), scale=dh ** -0.5, window=SWA_WINDOW,
                    sinks=sink_rows, tq=tq)
    return _tokens_major(ot, b, s)


def _xattn_kernel(q_ref, kv_ref, o_ref, *, scale):
    d = q_ref.shape[-1]
    hd = XATTN_HEAD_DIM
    for hh in range(XATTN_HEADS):
        q = q_ref[:, hh * hd:(hh + 1) * hd]
        k = kv_ref[:, hh * hd:(hh + 1) * hd]
        v = kv_ref[:, d + hh * hd:d + (hh + 1) * hd]
        s = lax.dot_general(q, k, (((1,), (1,)), ((), ())), preferred_element_type=F32) * scale
        m = jnp.max(s, axis=-1, keepdims=True)
        e = jnp.exp(s - m)
        p = e / jnp.sum(e, axis=-1, keepdims=True)
        o_ref[:, hh * hd:(hh + 1) * hd] = jnp.dot(p.astype(BF16), v, preferred_element_type=F32).astype(o_ref.dtype)


def _cross_attention(q, kv, b, s, tq=512):
    d = q.shape[-1]
    mlen = kv.shape[0] // b
    tq = min(tq, s)
    q3 = q.reshape(b, s, d)
    kv3 = kv.reshape(b, mlen, 2 * d)
    o = pl.pallas_call(
        functools.partial(_xattn_kernel, scale=XATTN_HEAD_DIM ** -0.5),
        out_shape=jax.ShapeDtypeStruct((b, s, d), BF16), grid=(b, s // tq),
        in_specs=[pl.BlockSpec((None, tq, d), lambda bi, i: (bi, i, 0)),
                  pl.BlockSpec((None, mlen, 2 * d), lambda bi, i: (bi, 0, 0))],
        out_specs=pl.BlockSpec((None, tq, d), lambda bi, i: (bi, i, 0)),
        compiler_params=_params("parallel", "parallel"), name="cross_attention",
    )(q3, kv3)
    return o.reshape(b * s, d)


def _swiglu_partial(xb, wg_ref, wu_ref, wo_ref):
    gte = jnp.dot(xb, wg_ref[...].astype(BF16), preferred_element_type=F32)
    up = jnp.dot(xb, wu_ref[...].astype(BF16), preferred_element_type=F32)
    act = gte * jax.nn.sigmoid(gte) * up
    return jnp.dot(act.astype(BF16), wo_ref[...].astype(BF16), preferred_element_type=F32)


def _ffn_ln_kernel(x_ref, wg_ref, wu_ref, wo_ref, g_ref, b_ref, o_ref, acc_ref):
    f = pl.program_id(1)

    @pl.when(f == 0)
    def _():
        acc_ref[...] = jnp.zeros_like(acc_ref)

    acc_ref[...] += _swiglu_partial(x_ref[...].astype(BF16), wg_ref, wu_ref, wo_ref)

    @pl.when(f == pl.num_programs(1) - 1)
    def _():
        o_ref[...] = _ln(DEEPNORM_ALPHA * x_ref[...] + acc_ref[...], g_ref[...], b_ref[...])


def _ffn_ln(h, w_in, w_out, g, b, tm=1024, tf=512):
    m, d = h.shape
    dff = w_out.shape[0]
    tm = min(tm, m)
    nf = dff // tf
    row = pl.BlockSpec((tm, d), lambda i, f: (i, 0))
    vec = pl.BlockSpec((1, d), lambda i, f: (0, 0))
    return pl.pallas_call(
        _ffn_ln_kernel, out_shape=jax.ShapeDtypeStruct((m, d), F32), grid=(m // tm, nf),
        in_specs=[row,
                  pl.BlockSpec((d, tf), lambda i, f: (0, f)),
                  pl.BlockSpec((d, tf), lambda i, f: (0, nf + f)),
                  pl.BlockSpec((tf, d), lambda i, f: (f, 0)),
                  vec, vec],
        out_specs=row, scratch_shapes=[pltpu.VMEM((tm, d), F32)],
        compiler_params=_params("parallel", "arbitrary"), name="ffn_ln",
    )(h, w_in, w_in, w_out, g.reshape(1, d), b.reshape(1, d))


def _router_kernel(x_ref, rh_ref, rl_ref, b_ref, comb_ref, sel_ref):
    x = x_ref[...]
    xh = x.astype(BF16)
    xl = (x - xh.astype(F32)).astype(BF16)
    rh, rl = rh_ref[...], rl_ref[...]
    lg = (jnp.dot(xh, rh, preferred_element_type=F32) + jnp.dot(xl, rh, preferred_element_type=F32)
          + jnp.dot(xh, rl, preferred_element_type=F32)) + b_ref[...]
    lane = lax.broadcasted_iota(jnp.int32, lg.shape, 1).astype(F32)
    lg = jnp.where(lane < N_EXPERTS, lg, NEG_INF)
    m1 = jnp.max(lg, axis=-1, keepdims=True)
    i1 = jnp.min(jnp.where(lg == m1, lane, float(LANES)), axis=-1, keepdims=True)
    lg2 = jnp.where(lane == i1, NEG_INF, lg)
    m2 = jnp.max(lg2, axis=-1, keepdims=True)
    i2 = jnp.min(jnp.where(lg2 == m2, lane, float(LANES)), axis=-1, keepdims=True)
    e2 = jnp.exp(m2 - m1)
    den = 1.0 + e2
    comb_ref[...] = jnp.where(lane == i1, 1.0 / den, 0.0) + jnp.where(lane == i2, e2 / den, 0.0)
    sel_ref[...] = jnp.where((lane == i1) | (lane == i2), 1.0, 0.0)


def _router(h, router, bias, tm=512):
    m, d = h.shape
    tm = min(tm, m)
    rp = jnp.pad(router.astype(F32), ((0, 0), (0, LANES - N_EXPERTS)))
    rh = rp.astype(BF16)
    rl = (rp - rh.astype(F32)).astype(BF16)
    bp = jnp.pad(bias.astype(F32), (0, LANES - N_EXPERTS)).reshape(1, LANES)
    row = pl.BlockSpec((tm, d), lambda i: (i, 0))
    wsp = pl.BlockSpec((d, LANES), lambda i: (0, 0))
    osp = pl.BlockSpec((tm, LANES), lambda i: (i, 0))
    return pl.pallas_call(
        _router_kernel, out_shape=[jax.ShapeDtypeStruct((m, LANES), F32)] * 2, grid=(m // tm,),
        in_specs=[row, wsp, wsp, pl.BlockSpec((1, LANES), lambda i: (0, 0))], out_specs=[osp, osp],
        compiler_params=_params("parallel"), name="moe_router",
    )(h, rh, rl, bp)


SCALAR_UNROLL = 8


def _slot_tokens_kernel(pos_ref, src_ref):
    t = pos_ref.shape[0] // 2

    def zero(i, c):
        src_ref[i] = 0
        return c

    lax.fori_loop(0, src_ref.shape[0], zero, 0, unroll=SCALAR_UNROLL)

    def body(i, c):
        src_ref[pos_ref[i]] = i
        src_ref[pos_ref[t + i]] = i
        return c

    lax.fori_loop(0, t, body, 0, unroll=SCALAR_UNROLL)


def _slot_tokens(pos2, tp):
    smem = pl.BlockSpec(memory_space=pltpu.SMEM)
    return pl.pallas_call(
        _slot_tokens_kernel, out_shape=jax.ShapeDtypeStruct((tp,), jnp.int32),
        in_specs=[smem], out_specs=smem, name="moe_slot_tokens",
    )(pos2)


def _moe_ffn_kernel(te_ref, tv_ref, x_ref, wg_ref, wu_ref, wo_ref, o_ref, acc_ref):
    i = pl.program_id(0)
    f = pl.program_id(1)

    @pl.when(f == 0)
    def _():
        acc_ref[...] = jnp.zeros_like(acc_ref)

    @pl.when(tv_ref[i] > 0)
    def _():
        acc_ref[...] += _swiglu_partial(x_ref[...].astype(BF16), wg_ref, wu_ref, wo_ref)

    @pl.when(f == pl.num_programs(1) - 1)
    def _():
        o_ref[...] = acc_ref[...].astype(o_ref.dtype)


def _moe_ffn(xs, w_in, w_out, tile_expert, tile_valid, tm, tf=512):
    tp, d = xs.shape
    dff = w_out.shape[1]
    nf = dff // tf
    last = nf - 1

    def fblk(f, tv, i):
        return jnp.where(tv[i] > 0, f, last)

    grid_spec = pltpu.PrefetchScalarGridSpec(
        num_scalar_prefetch=2, grid=(tp // tm, nf),
        in_specs=[pl.BlockSpec((tm, d), lambda i, f, te, tv: (i, 0)),
                  pl.BlockSpec((None, d, tf), lambda i, f, te, tv: (te[i], 0, fblk(f, tv, i))),
                  pl.BlockSpec((None, d, tf), lambda i, f, te, tv: (te[i], 0, nf + fblk(f, tv, i))),
                  pl.BlockSpec((None, tf, d), lambda i, f, te, tv: (te[i], fblk(f, tv, i), 0))],
        out_specs=pl.BlockSpec((tm, d), lambda i, f, te, tv: (i, 0)),
        scratch_shapes=[pltpu.VMEM((tm, d), F32)])
    return pl.pallas_call(
        _moe_ffn_kernel, out_shape=jax.ShapeDtypeStruct((tp, d), BF16), grid_spec=grid_spec,
        compiler_params=_params("parallel", "arbitrary"), name="moe_ffn",
    )(tile_expert, tile_valid, xs, w_in, w_in, w_out)


def _moe_layer(h, router, bias, w_in, w_out, g, b, tm=512):
    t, d = h.shape
    tm = min(tm, t)
    comb, sel = _router(h, router, bias)
    comb, sel = comb[:, :N_EXPERTS], sel[:, :N_EXPERTS] > 0.5

    seli = sel.astype(jnp.int32)
    cnt = jnp.sum(seli, axis=0)
    ntile = (cnt + tm - 1) // tm
    tile_end = jnp.cumsum(ntile)
    row_start = (tile_end - ntile) * tm
    rank = jnp.cumsum(seli, axis=0) - seli
    n_tiles = (t * TOP_K) // tm + N_EXPERTS
    tp = n_tiles * tm
    pos = jnp.where(sel, row_start[None, :] + rank, tp)
    tiles = jnp.arange(n_tiles, dtype=jnp.int32)
    tile_valid = (tiles < tile_end[-1]).astype(jnp.int32)
    tile_expert = jnp.sum((tiles[:, None] >= tile_end[None, :]).astype(jnp.int32), axis=1)
    last_e = jnp.sum((tile_end[-1] - 1 >= tile_end).astype(jnp.int32))
    tile_expert = jnp.where(tile_valid > 0, tile_expert, last_e).astype(jnp.int32)
    p_lo = jnp.min(pos, axis=1)
    p_hi = jnp.max(jnp.where(sel, pos, -1), axis=1)
    w2 = jnp.stack([jnp.sum(jnp.where(pos == p_lo[:, None], comb, 0.0), axis=1),
                    jnp.sum(jnp.where(pos == p_hi[:, None], comb, 0.0), axis=1)], axis=1)
    pos2 = jnp.concatenate([p_lo, p_hi])
    xs = jnp.take(h, _slot_tokens(pos2, tp), axis=0)
    ys = _moe_ffn(xs, w_in, w_out, tile_expert, tile_valid, tm)
    return _combine_ln(h, jnp.take(ys, pos2, axis=0), w2, g, b)


def kernel(x, mem, positions, l0_nsa_w_in, l0_nsa_cmp_pe, l0_nsa_cmp_w1, l0_nsa_cmp_w2, l0_nsa_w_out, l0_ln1_g, l0_ln1_b, l0_xq, l0_xkv, l0_xo, l0_ln2_g, l0_ln2_b, l0_ffn_w_in, l0_ffn_w_out, l0_ln3_g, l0_ln3_b, l1_mla_w_down, l1_mla_q_norm, l1_mla_kv_norm, l1_mla_w_uq, l1_mla_w_ukv, l1_mla_w_out, l1_ln1_g, l1_ln1_b, l1_xq, l1_xkv, l1_xo, l1_ln2_g, l1_ln2_b, l1_moe_router, l1_moe_bias, l1_moe_w_in, l1_moe_w_out, l1_ln3_g, l1_ln3_b, l2_moba_w_in, l2_moba_w_out, l2_ln1_g, l2_ln1_b, l2_xq, l2_xkv, l2_xo, l2_ln2_g, l2_ln2_b, l2_ffn_w_in, l2_ffn_w_out, l2_ln3_g, l2_ln3_b, l3_swa_w_in, l3_swa_sinks, l3_swa_w_out, l3_ln1_g, l3_ln1_b, l3_xq, l3_xkv, l3_xo, l3_ln2_g, l3_ln2_b, l3_moe_router, l3_moe_bias, l3_moe_w_in, l3_moe_w_out, l3_ln3_g, l3_ln3_b):
    b, s, d = x.shape
    t = b * s
    tab_p = _rope_tables(positions, ROPE_DIM, HEAD_DIM)
    tab_m = _rope_tables(positions, MLA_ROPE_DIM, MLA_ROPE_DIM)
    mem2 = mem.reshape(-1, d)

    mixers = [
        (lambda h: _nsa_mixer(h, b, s, tab_p, l0_nsa_w_in, l0_nsa_cmp_pe, l0_nsa_cmp_w1, l0_nsa_cmp_w2), l0_nsa_w_out),
        (lambda h: _mla_mixer(h, b, s, tab_m, l1_mla_w_down, l1_mla_q_norm, l1_mla_kv_norm, l1_mla_w_uq, l1_mla_w_ukv),
         l1_mla_w_out),
        (lambda h: _moba_mixer(h, b, s, tab_p, l2_moba_w_in), l2_moba_w_out),
        (lambda h: _swa_mixer(h, b, s, tab_p, l3_swa_w_in, l3_swa_sinks), l3_swa_w_out),
    ]
    ln1 = [(l0_ln1_g, l0_ln1_b), (l1_ln1_g, l1_ln1_b), (l2_ln1_g, l2_ln1_b), (l3_ln1_g, l3_ln1_b)]
    xattn = [(l0_xq, l0_xkv, l0_xo), (l1_xq, l1_xkv, l1_xo), (l2_xq, l2_xkv, l2_xo), (l3_xq, l3_xkv, l3_xo)]
    ln2 = [(l0_ln2_g, l0_ln2_b), (l1_ln2_g, l1_ln2_b), (l2_ln2_g, l2_ln2_b), (l3_ln2_g, l3_ln2_b)]
    ln3 = [(l0_ln3_g, l0_ln3_b), (l1_ln3_g, l1_ln3_b), (l2_ln3_g, l2_ln3_b), (l3_ln3_g, l3_ln3_b)]
    ffns = [
        lambda h, g, bb: _ffn_ln(h, l0_ffn_w_in, l0_ffn_w_out, g, bb),
        lambda h, g, bb: _moe_layer(h, l1_moe_router, l1_moe_bias, l1_moe_w_in, l1_moe_w_out, g, bb),
        lambda h, g, bb: _ffn_ln(h, l2_ffn_w_in, l2_ffn_w_out, g, bb),
        lambda h, g, bb: _moe_layer(h, l3_moe_router, l3_moe_bias, l3_moe_w_in, l3_moe_w_out, g, bb),
    ]

    h = x.reshape(t, d)
    for i in range(DEPTH):
        mixer, w_out = mixers[i]
        h = _mm_res_ln(mixer(h), w_out.astype(BF16), h, *ln1[i])
        w_q, w_kv, w_o = xattn[i]
        (q,) = _proj(h, [w_q.astype(BF16)], [BF16])
        (kv,) = _proj(mem2, [w_kv.astype(BF16)], [BF16])
        h = _mm_res_ln(_cross_attention(q, kv, b, s), w_o.astype(BF16), h, *ln2[i])
        h = ffns[i](h, *ln3[i])
    return h.reshape(b, s, d)
```

```python
import functools

import numpy as np
import jax
import jax.numpy as jnp
from jax import lax
from jax.experimental import pallas as pl
from jax.experimental.pallas import tpu as pltpu

BF16 = jnp.bfloat16
F32 = jnp.float32

D_MODEL = 1024
HEAD_DIM = 64
N_HEADS = D_MODEL // HEAD_DIM
ROPE_THETA = 500000.0
ROPE_DIM = HEAD_DIM // 4

NSA_KV_GROUPS = 4
NSA_CMP_BLOCK = 32
NSA_CMP_STRIDE = 16
NSA_SLC_BLOCK = 64
NSA_SLC_TOPN = 8
NSA_WINDOW = 512
NSA_FORCE_SCORE = 1e4

MLA_Q_RANK = D_MODEL // 4
MLA_KV_RANK = D_MODEL // 8
MLA_NOPE_DIM = 64
MLA_ROPE_DIM = 32
MLA_V_DIM = 64

MOBA_BLOCK = 256
MOBA_TOPK = 3

SWA_KV_HEADS = 2
SWA_WINDOW = 128

XATTN_HEADS = 4
XATTN_HEAD_DIM = D_MODEL // XATTN_HEADS

D_FF = (D_MODEL * 7) // 2
N_EXPERTS = 8
TOP_K = 2

DEPTH = 4
DEEPNORM_ALPHA = (2.0 * DEPTH) ** 0.25
LN_EPS = 1e-5
RMS_EPS = 1e-6

LANES = 128
SUBLANES = 8
NEG_INF = float("-inf")
M_INIT = -1e30
VMEM_LIMIT = 48 * 1024 * 1024


def _params(*sem):
    return pltpu.CompilerParams(dimension_semantics=sem, vmem_limit_bytes=VMEM_LIMIT)


def _ln(z, g, b):
    mu = jnp.mean(z, axis=-1, keepdims=True)
    zc = z - mu
    var = jnp.mean(zc * zc, axis=-1, keepdims=True)
    return zc * lax.rsqrt(var + LN_EPS) * g + b


def _div_pow2(x, d):
    assert d & (d - 1) == 0
    return x >> (d.bit_length() - 1)


def _is_pow2(v):
    m, _ = np.frexp(v)
    return m == 0.5


def _proj_kernel(*refs, n_out, rope, shift, rms, chunk):
    x_ref = refs[0]
    pos = 1
    if rms:
        g_ref = refs[pos]
        pos += 1
    w_refs = refs[pos:pos + n_out]
    pos += n_out
    if any(rope):
        c_ref, s1_ref, s2_ref = refs[pos:pos + 3]
        pos += 3
    o_refs = refs[pos:pos + n_out]

    x = x_ref[...]
    if rms:
        xf = x.astype(F32)
        xf = xf * lax.rsqrt(jnp.mean(xf * xf, axis=-1, keepdims=True) + RMS_EPS) * g_ref[...]
        xb = xf.astype(BF16)
    else:
        xb = x.astype(BF16)
    for i in range(n_out):
        n = w_refs[i].shape[1]
        for c0 in range(0, n, chunk):
            cw = min(chunk, n - c0)
            acc = jnp.dot(xb, w_refs[i][:, c0:c0 + cw], preferred_element_type=F32)
            if rope[i]:
                reps = cw // LANES
                c, s1, s2 = c_ref[...], s1_ref[...], s2_ref[...]
                if reps > 1:
                    c = jnp.concatenate([c] * reps, axis=1)
                    s1 = jnp.concatenate([s1] * reps, axis=1)
                    s2 = jnp.concatenate([s2] * reps, axis=1)
                acc = acc * c + pltpu.roll(acc, shift, 1) * s1 + pltpu.roll(acc, cw - shift, 1) * s2
            o_refs[i][:, c0:c0 + cw] = acc.astype(o_refs[i].dtype)


def _proj(x, ws, out_dtypes, *, rope=None, tables=None, shift=0, rms_g=None, tm=512):
    m, k = x.shape
    n_out = len(ws)
    rope = tuple(rope) if rope is not None else (False,) * n_out
    tm = min(tm, m)
    assert m % tm == 0
    in_specs = [pl.BlockSpec((tm, k), lambda i: (i, 0))]
    args = [x]
    if rms_g is not None:
        in_specs.append(pl.BlockSpec((1, k), lambda i: (0, 0)))
        args.append(rms_g.reshape(1, k).astype(F32))
    for w in ws:
        assert w.shape[1] % LANES == 0
        in_specs.append(pl.BlockSpec(w.shape, lambda i: (0, 0)))
        args.append(w)
    if any(rope):
        for t in tables:
            in_specs.append(pl.BlockSpec((tm, LANES), lambda i: (i, 0)))
            args.append(t)
    out_shape = [jax.ShapeDtypeStruct((m, w.shape[1]), dt) for w, dt in zip(ws, out_dtypes)]
    out_specs = [pl.BlockSpec((tm, w.shape[1]), lambda i: (i, 0)) for w in ws]
    kern = functools.partial(_proj_kernel, n_out=n_out, rope=rope, shift=shift,
                             rms=rms_g is not None, chunk=2 * LANES)
    return pl.pallas_call(kern, out_shape=out_shape, grid=(m // tm,), in_specs=in_specs,
                          out_specs=out_specs, compiler_params=_params("parallel"), name="proj")(*args)


def _rope_tables(positions, rot_dim, period):
    half = rot_dim // 2
    inv = ROPE_THETA ** (-jnp.arange(0, rot_dim, 2, dtype=F32) / rot_dim)
    ang = positions.astype(F32).reshape(-1)[:, None] * inv
    cos, sin = jnp.cos(ang), jnp.sin(ang)
    t = ang.shape[0]
    zero_h = jnp.zeros((t, half), F32)
    rest = period - rot_dim
    c = jnp.concatenate([cos, cos, jnp.ones((t, rest), F32)], axis=1)
    s1 = jnp.concatenate([zero_h, sin, jnp.zeros((t, rest), F32)], axis=1)
    s2 = jnp.concatenate([-sin, zero_h, jnp.zeros((t, rest), F32)], axis=1)
    reps = LANES // period
    return tuple(jnp.tile(a, (1, reps)) for a in (c, s1, s2))


def _mm_res_ln_kernel(a_ref, w_ref, h_ref, g_ref, b_ref, o_ref):
    y = jnp.dot(a_ref[...].astype(BF16), w_ref[...], preferred_element_type=F32)
    o_ref[...] = _ln(DEEPNORM_ALPHA * h_ref[...] + y, g_ref[...], b_ref[...])


def _mm_res_ln(a, w, h, g, b, tm=512):
    m, k = a.shape
    d = w.shape[1]
    tm = min(tm, m)
    return pl.pallas_call(
        _mm_res_ln_kernel,
        out_shape=jax.ShapeDtypeStruct((m, d), F32),
        grid=(m // tm,),
        in_specs=[pl.BlockSpec((tm, k), lambda i: (i, 0)),
                  pl.BlockSpec((k, d), lambda i: (0, 0)),
                  pl.BlockSpec((tm, d), lambda i: (i, 0)),
                  pl.BlockSpec((1, d), lambda i: (0, 0)),
                  pl.BlockSpec((1, d), lambda i: (0, 0))],
        out_specs=pl.BlockSpec((tm, d), lambda i: (i, 0)),
        compiler_params=_params("parallel"), name="mm_res_ln",
    )(a, w, h, g.reshape(1, d), b.reshape(1, d))


def _combine_ln_kernel(h_ref, a_ref, b2_ref, w_ref, g_ref, b_ref, o_ref):
    w = w_ref[...]
    y = w[:, 0:1] * a_ref[...].astype(F32) + w[:, 1:2] * b2_ref[...].astype(F32)
    o_ref[...] = _ln(DEEPNORM_ALPHA * h_ref[...] + y, g_ref[...], b_ref[...])


def _combine_ln(h, y2, w2, g, b, tm=512):
    m, d = h.shape
    tm = min(tm, m)
    nt = m // tm
    row = pl.BlockSpec((tm, d), lambda i: (i, 0))
    vec = pl.BlockSpec((1, d), lambda i: (0, 0))
    return pl.pallas_call(
        _combine_ln_kernel, out_shape=jax.ShapeDtypeStruct((m, d), F32), grid=(nt,),
        in_specs=[row, row, pl.BlockSpec((tm, d), lambda i: (nt + i, 0)),
                  pl.BlockSpec((tm, w2.shape[1]), lambda i: (i, 0)), vec, vec],
        out_specs=row, compiler_params=_params("parallel"), name="combine_ln",
    )(h, y2, y2, w2, g.reshape(1, d), b.reshape(1, d))


def _flash_t(qts, k_refs, vt_refs, q0, *, tq, tk, scale, window=None, m0s=None, bias_refs=None, blk=None):
    n = len(qts)
    dv = vt_refs[0].shape[1]
    fold = _is_pow2(scale)
    if fold:
        qts = [q * jnp.asarray(scale, q.dtype) for q in qts]
    qpos = q0 + lax.broadcasted_iota(jnp.int32, (1, tq), 1)
    carry = []
    for c in range(n):
        if m0s is None:
            carry += [jnp.full((1, tq), M_INIT, F32), jnp.zeros((1, tq), F32)]
        else:
            carry += [m0s[c], jnp.ones((1, tq), F32)]
        carry.append(jnp.zeros((dv, tq), F32))

    def make_body(masked):
        def body(j, carry):
            start = pl.multiple_of(j * tk, tk)
            if masked:
                delta = qpos - (start + lax.broadcasted_iota(jnp.int32, (tk, 1), 0))
                mask = delta >= 0
                if window is not None:
                    mask = mask & (delta < window)
            nxt = pl.multiple_of(jnp.minimum(j + 1, hi) * tk, tk)
            k_blocks, v_blocks, biases = {}, {}, {}
            out = []
            for c in range(n):
                m, l, acc, s = carry[4 * c:4 * c + 4]
                if id(k_refs[c]) not in k_blocks:
                    k_blocks[id(k_refs[c])] = k_refs[c][pl.ds(nxt, tk), :]
                    v_blocks[id(k_refs[c])] = vt_refs[c][j]
                s_next = jnp.dot(k_blocks[id(k_refs[c])], qts[c], preferred_element_type=F32)
                if not fold:
                    s = s * scale
                if bias_refs is not None:
                    br = bias_refs[c]
                    if id(br) not in biases:
                        per = tk // blk
                        bias = br[j, per - 1:per, :]
                        if per > 1:
                            sub = lax.broadcasted_iota(jnp.int32, (tk, tq), 0)
                            bias = jnp.broadcast_to(bias, (tk, tq))
                            for r in range(per - 2, -1, -1):
                                bias = jnp.where(sub < (r + 1) * blk, br[j, r:r + 1, :], bias)
                        biases[id(br)] = bias
                    s = s + biases[id(br)]
                if masked:
                    s = jnp.where(mask, s, NEG_INF)
                m_new = jnp.maximum(m, jnp.max(s, axis=0, keepdims=True))
                a = jnp.exp(m - m_new)
                p = jnp.exp(s - m_new)
                l_new = a * l + jnp.sum(p, axis=0, keepdims=True)
                acc_new = a * acc + jnp.dot(v_blocks[id(k_refs[c])], p.astype(BF16), preferred_element_type=F32)
                out += [m_new, l_new, acc_new, s_next]
            return tuple(out)
        return body

    diag = _div_pow2(q0, tk)
    hi = _div_pow2(q0 + tq - 1, tk)
    if window is None:
        lo = lo_full = 0
    else:
        lo = _div_pow2(jnp.maximum(q0 - (window - 1), 0), tk)
        lo_full = jnp.minimum(_div_pow2(jnp.maximum(q0 + tq - window + tk - 1, 0), tk), diag)
    first = 0 if window is None else pl.multiple_of(lo * tk, tk)
    carry = tuple(v for c in range(n) for v in (
        *carry[3 * c:3 * c + 3],
        jnp.dot(k_refs[c][pl.ds(first, tk), :], qts[c], preferred_element_type=F32)))
    if window is not None:
        carry = lax.fori_loop(lo, lo_full, make_body(True), carry)
    carry = lax.fori_loop(lo_full, diag, make_body(False), carry)
    carry = lax.fori_loop(diag, hi + 1, make_body(True), carry)
    return [carry[4 * c + 2] / jnp.maximum(carry[4 * c + 1], 1e-30) for c in range(n)]


def _topk_rows(score, valid, k):
    row = lax.broadcasted_iota(jnp.int32, score.shape, 0)
    rank = jnp.zeros(score.shape, F32)
    for j in range(score.shape[0]):
        r = score[j:j + 1, :]
        beats = (r > score) | ((r == score) & (row > j))
        rank = rank + jnp.where(beats, 1.0, 0.0)
    return jnp.where(valid & (rank < k), 1.0, 0.0)


def _fill_bias(bias_ref, sel, per):
    bias = jnp.where(sel > 0.5, 0.0, NEG_INF)
    for jb in range(bias_ref.shape[0]):
        bias_ref[jb, 0:per, :] = bias[jb * per:(jb + 1) * per, :]


def _attn_kernel(*refs, hb, rep, tq, tk, scale, window, has_sink):
    if has_sink:
        qt_ref, k_ref, vt_ref, sink_ref, o_ref = refs
    else:
        qt_ref, k_ref, vt_ref, o_ref = refs
    q0 = pl.program_id(2) * tq
    heads = [(hh, r) for hh in range(hb) for r in range(rep)]
    k_views = [k_ref.at[hh] for hh in range(hb)]
    vt_views = [vt_ref.at[hh] for hh in range(hb)]
    outs = _flash_t([qt_ref[hh, r] for hh, r in heads], [k_views[hh] for hh, _ in heads],
                    [vt_views[hh] for hh, _ in heads], q0, tq=tq, tk=tk, scale=scale, window=window,
                    m0s=[sink_ref[hh, r] for hh, r in heads] if has_sink else None)
    for (hh, r), o in zip(heads, outs):
        o_ref[hh, r] = o.astype(o_ref.dtype)


def _attention(qt, k, vt, *, scale, window=None, sinks=None, tq=128, hb=1):
    b, g, rep, dk, s = qt.shape
    nblk, dv, tk = vt.shape[2:]
    tq = min(tq, s)
    assert g % hb == 0
    in_specs = [pl.BlockSpec((None, hb, rep, dk, tq), lambda bi, gi, i: (bi, gi, 0, 0, i)),
                pl.BlockSpec((None, hb, s, dk), lambda bi, gi, i: (bi, gi, 0, 0)),
                pl.BlockSpec((None, hb, nblk, dv, tk), lambda bi, gi, i: (bi, gi, 0, 0, 0))]
    args = [qt, k, vt]
    if sinks is not None:
        in_specs.append(pl.BlockSpec((hb, rep, 1, tq), lambda bi, gi, i: (gi, 0, 0, 0)))
        args.append(sinks)
    kern = functools.partial(_attn_kernel, hb=hb, rep=rep, tq=tq, tk=tk, scale=scale, window=window,
                             has_sink=sinks is not None)
    return pl.pallas_call(
        kern, out_shape=jax.ShapeDtypeStruct((b, g, rep, dv, s), BF16), grid=(b, g // hb, s // tq),
        in_specs=in_specs,
        out_specs=pl.BlockSpec((None, hb, rep, dv, tq), lambda bi, gi, i: (bi, gi, 0, 0, i)),
        compiler_params=_params("parallel", "parallel", "parallel"), name="attention",
    )(*args)


def _heads_first(t, b, s, h):
    return t.reshape(b, s, h, t.shape[-1] // h).transpose(0, 2, 1, 3)


def _heads_t(t, b, s, h):
    return t.reshape(b, s, h, t.shape[-1] // h).transpose(0, 2, 3, 1)


def _vt_blocks(t, b, s, h, tk):
    return t.reshape(b, s // tk, tk, h, t.shape[-1] // h).transpose(0, 3, 1, 4, 2)


def _tokens_major(ot, b, s):
    return ot.reshape(b, -1, s).transpose(0, 2, 1).reshape(b * s, -1)


def _gelu_tanh(x):
    return 0.5 * x * (1.0 + jnp.tanh(np.sqrt(2.0 / np.pi).astype(np.float32) * (x + 0.044715 * (x * x * x))))


def _nsa_compress_kernel(ak_ref, av_ref, pe_ref, w1_ref, w2_ref, kc_ref, vc_ref):
    half = ak_ref.shape[-1]
    ncp = ak_ref.shape[0]
    for j, (a_ref, o_ref) in enumerate(((ak_ref, kc_ref), (av_ref, vc_ref))):
        a = a_ref[...].astype(F32)
        lo = (a + pe_ref[j, 0:1, :]).astype(BF16)
        hi = (a + pe_ref[j, 1:2, :]).astype(BF16)
        h1 = jnp.dot(lo, w1_ref[j, :half, :], preferred_element_type=F32)
        h2 = jnp.dot(hi, w1_ref[j, half:, :], preferred_element_type=F32)
        h = _gelu_tanh(h1 + pltpu.roll(h2, ncp - 1, 0))
        o_ref[...] = jnp.dot(h.astype(BF16), w2_ref[j], preferred_element_type=F32).astype(o_ref.dtype)


def _nsa_compress(ak, av, pe2, w1, w2):
    b, g, ncp, wid = ak.shape
    dh = w2.shape[-1]
    blk = pl.BlockSpec((None, None, ncp, wid), lambda bi, gi: (bi, gi, 0, 0))
    full = lambda a: pl.BlockSpec(a.shape, lambda bi, gi: (0,) * a.ndim)
    out = pl.BlockSpec((None, None, ncp, dh), lambda bi, gi: (bi, gi, 0, 0))
    return pl.pallas_call(
        _nsa_compress_kernel,
        out_shape=[jax.ShapeDtypeStruct((b, g, ncp, dh), BF16)] * 2,
        grid=(b, g), in_specs=[blk, blk, full(pe2), full(w1), full(w2)], out_specs=[out, out],
        compiler_params=_params("parallel", "parallel"), name="nsa_compress",
    )(ak, av, pe2, w1, w2)


def _nsa_attn_kernel(qt_ref, kc_ref, vct_ref, ks_ref, vst_ref, kw_ref, vwt_ref, gl_ref, ovlt_ref, o_ref,
                     bias_ref, *, rep, tq, tk, nc, n_sel, scale):
    q0 = pl.program_id(2) * tq
    rows = rep * tq
    qts = [qt_ref[r] for r in range(rep)]
    qt = jnp.concatenate(qts, axis=1)
    row_pos = q0 + (lax.broadcasted_iota(jnp.int32, (1, rows), 1) & (tq - 1))

    ncp = kc_ref.shape[0]
    s = jnp.dot(kc_ref[...], qt * jnp.asarray(scale, qt.dtype), preferred_element_type=F32)
    cidx = lax.broadcasted_iota(jnp.int32, (ncp, 1), 0)
    cmask = (cidx * NSA_CMP_STRIDE + (NSA_CMP_BLOCK - 1) <= row_pos) & (cidx < nc)
    s = jnp.where(cmask, s, NEG_INF)
    m = jnp.max(s, axis=0, keepdims=True)
    m = jnp.where(m == NEG_INF, 0.0, m)
    e = jnp.exp(s - m)
    p = e / jnp.maximum(jnp.sum(e, axis=0, keepdims=True), 1e-30)
    o_cmp = jnp.dot(vct_ref[...], p.astype(BF16), preferred_element_type=F32)

    psum = p[:, 0:tq]
    for r in range(1, rep):
        psum = psum + p[:, r * tq:(r + 1) * tq]
    p_hi = psum.astype(BF16)
    p_lo = (psum - p_hi.astype(F32)).astype(BF16)
    ovlt = ovlt_ref[...]
    imp = (jnp.dot(ovlt, p_hi, preferred_element_type=F32)
           + jnp.dot(ovlt, p_lo, preferred_element_type=F32))
    nb = imp.shape[0]
    blk = lax.broadcasted_iota(jnp.int32, (nb, tq), 0)
    cur = _div_pow2(q0 + lax.broadcasted_iota(jnp.int32, (nb, tq), 1), NSA_SLC_BLOCK)
    forced = (blk == 0) | (blk == cur) | (blk == cur - 1)
    valid = blk <= cur
    score = jnp.where(valid, jnp.where(forced, NSA_FORCE_SCORE, imp), NEG_INF)
    _fill_bias(bias_ref, _topk_rows(score, valid, n_sel), tk // NSA_SLC_BLOCK)

    o_slc = _flash_t(qts, [ks_ref] * rep, [vst_ref] * rep, q0, tq=tq, tk=tk, scale=scale,
                     bias_refs=[bias_ref] * rep, blk=NSA_SLC_BLOCK)
    o_win = _flash_t(qts, [kw_ref] * rep, [vwt_ref] * rep, q0, tq=tq, tk=tk, scale=scale, window=NSA_WINDOW)

    for r in range(rep):
        gate = jax.nn.sigmoid(gl_ref[r])
        o = (gate[0:1, :] * o_cmp[:, r * tq:(r + 1) * tq] + gate[1:2, :] * o_slc[r] + gate[2:3, :] * o_win[r])
        o_ref[r] = o.astype(o_ref.dtype)


def _nsa_attention(qt, kc, vct, ks, vst, kw, vwt, gl, ovlt, *, nc, tq=128):
    b, g, rep, dh, s = qt.shape
    ncp = kc.shape[2]
    nblk, _, tk = vst.shape[2:]
    nb = s // NSA_SLC_BLOCK
    tq = min(tq, s)
    qspec = pl.BlockSpec((None, None, rep, dh, tq), lambda bi, gi, i: (bi, gi, 0, 0, i))
    kspec = pl.BlockSpec((None, None, s, dh), lambda bi, gi, i: (bi, gi, 0, 0))
    vspec = pl.BlockSpec((None, None, nblk, dh, tk), lambda bi, gi, i: (bi, gi, 0, 0, 0))
    kern = functools.partial(_nsa_attn_kernel, rep=rep, tq=tq, tk=tk, nc=nc,
                             n_sel=min(NSA_SLC_TOPN, nb), scale=dh ** -0.5)
    return pl.pallas_call(
        kern, out_shape=jax.ShapeDtypeStruct((b, g, rep, dh, s), BF16), grid=(b, g, s // tq),
        in_specs=[qspec,
                  pl.BlockSpec((None, None, ncp, dh), lambda bi, gi, i: (bi, gi, 0, 0)),
                  pl.BlockSpec((None, None, dh, ncp), lambda bi, gi, i: (bi, gi, 0, 0)),
                  kspec, vspec, kspec, vspec,
                  pl.BlockSpec((None, None, rep, gl.shape[3], tq), lambda bi, gi, i: (bi, gi, 0, 0, i)),
                  pl.BlockSpec(ovlt.shape, lambda bi, gi, i: (0, 0))],
        out_specs=qspec,
        scratch_shapes=[pltpu.VMEM((nblk, SUBLANES, tq), F32)],
        compiler_params=_params("parallel", "parallel", "parallel"), name="nsa_attention",
    )(qt, kc, vct, ks, vst, kw, vwt, gl, ovlt)


def _nsa_mixer(h, b, s, tab_p, w_in, cmp_pe, cmp_w1, cmp_w2):
    hn, g, dh = N_HEADS, NSA_KV_GROUPS, HEAD_DIM
    rep = hn // g
    gd = g * dh
    q_end = hn * dh
    tk = min(128, s)
    kv = lambda j: w_in[:, q_end + j * gd:q_end + (j + 1) * gd]
    w_rope = jnp.concatenate([w_in[:, :q_end], kv(0), kv(2), kv(4)], axis=1).astype(BF16)
    w_plain = jnp.concatenate([kv(1), kv(3), kv(5)], axis=1).astype(BF16)
    n_gate = 3 * hn
    w_gate = jnp.pad(w_in[:, q_end + 6 * gd:], ((0, 0), (0, LANES - n_gate))).astype(BF16)
    roped, plain, gates = _proj(h, [w_rope, w_plain, w_gate], [BF16, BF16, F32],
                                rope=[True, False, False], tables=tab_p, shift=ROPE_DIM // 2)
    qt = _heads_t(roped[:, :q_end], b, s, hn).reshape(b, g, rep, dh, s)
    k_c, k_s, k_w = [_heads_first(roped[:, q_end + j * gd:q_end + (j + 1) * gd], b, s, g) for j in range(3)]
    v_c = _heads_first(plain[:, :gd], b, s, g)
    vst, vwt = [_vt_blocks(plain[:, j * gd:(j + 1) * gd], b, s, g, tk) for j in (1, 2)]
    gl = gates[:, :n_gate].reshape(b, s, 3, g, rep).transpose(0, 3, 4, 2, 1)

    nc = (s - NSA_CMP_BLOCK) // NSA_CMP_STRIDE + 1
    ncp = s // NSA_CMP_STRIDE
    half = NSA_CMP_STRIDE * dh
    ak = k_c.reshape(b, g, ncp, half)
    av = v_c.reshape(b, g, ncp, half)
    pe2 = cmp_pe.reshape(2, 2, half).astype(F32)
    kc, vc = _nsa_compress(ak, av, pe2, cmp_w1.astype(BF16), cmp_w2.astype(BF16))
    vct = vc.transpose(0, 1, 3, 2)

    nb = s // NSA_SLC_BLOCK
    c_start = np.arange(ncp) * NSA_CMP_STRIDE
    b_start = np.arange(nb) * NSA_SLC_BLOCK
    overlap = np.clip(np.minimum(c_start[:, None] + NSA_CMP_BLOCK, b_start[None, :] + NSA_SLC_BLOCK)
                      - np.maximum(c_start[:, None], b_start[None, :]), 0, None) / NSA_CMP_BLOCK
    overlap[nc:] = 0.0
    ovlt = jnp.asarray(overlap.T, BF16)
    ot = _nsa_attention(qt, kc, vct, k_s, vst, k_w, vwt, gl, ovlt, nc=nc)
    return _tokens_major(ot, b, s)


def _mla_mixer(h, b, s, tab_m, w_down, q_norm, kv_norm, w_uq, w_ukv):
    hn = N_HEADS
    qr_, kvr = MLA_Q_RANK, MLA_KV_RANK
    w_kr = jnp.pad(w_down[:, qr_ + kvr:], ((0, 0), (0, LANES - MLA_ROPE_DIM)))
    cq, ckv, kr = _proj(h, [w_down[:, :qr_].astype(BF16), w_down[:, qr_:qr_ + kvr].astype(BF16),
                            w_kr.astype(BF16)], [F32, F32, BF16],
                        rope=[False, False, True], tables=tab_m, shift=MLA_ROPE_DIM // 2)
    w_uq3 = w_uq.reshape(qr_, hn, MLA_NOPE_DIM + MLA_ROPE_DIM)
    w_qn = w_uq3[:, :, :MLA_NOPE_DIM].reshape(qr_, hn * MLA_NOPE_DIM).astype(BF16)
    w_qr = w_uq3[:, :, MLA_NOPE_DIM:].reshape(qr_, hn * MLA_ROPE_DIM).astype(BF16)
    qn, qr = _proj(cq, [w_qn, w_qr], [BF16, BF16], rope=[False, True], tables=tab_m,
                   shift=MLA_ROPE_DIM // 2, rms_g=q_norm)
    w_ukv3 = w_ukv.reshape(kvr, hn, MLA_NOPE_DIM + MLA_V_DIM)
    w_kn = w_ukv3[:, :, :MLA_NOPE_DIM].reshape(kvr, hn * MLA_NOPE_DIM).astype(BF16)
    w_v = w_ukv3[:, :, MLA_NOPE_DIM:].reshape(kvr, hn * MLA_V_DIM).astype(BF16)
    kn, v = _proj(ckv, [w_kn, w_v], [BF16, BF16], rms_g=kv_norm)

    tk = min(256, s)
    dpad = LANES - MLA_NOPE_DIM - MLA_ROPE_DIM
    qt = jnp.concatenate([_heads_t(qn, b, s, hn), _heads_t(qr, b, s, hn),
                          jnp.zeros((b, hn, dpad, s), BF16)], axis=2)
    k_rope = jnp.broadcast_to(kr[:, :MLA_ROPE_DIM].reshape(b, 1, s, MLA_ROPE_DIM), (b, hn, s, MLA_ROPE_DIM))
    k = jnp.concatenate([_heads_first(kn, b, s, hn), k_rope, jnp.zeros((b, hn, s, dpad), BF16)], axis=-1)
    scale = (MLA_NOPE_DIM + MLA_ROPE_DIM) ** -0.5
    ot = _attention(qt[:, :, None], k, _vt_blocks(v, b, s, hn, tk), scale=scale, tq=256, hb=4)
    return _tokens_major(ot, b, s)


def _moba_kernel(qt_ref, k_ref, vt_ref, o_ref, bias_ref, *, hb, tq, n_sel, scale):
    q0 = pl.program_id(2) * tq
    s_len, dh = k_ref.shape[1:]
    nb = s_len // MOBA_BLOCK
    nbp = 2 * SUBLANES
    sub = lax.broadcasted_iota(jnp.int32, (nbp, dh), 0)
    blk = lax.broadcasted_iota(jnp.int32, (nb, tq), 0)
    own = _div_pow2(q0 + lax.broadcasted_iota(jnp.int32, (nb, tq), 1), MOBA_BLOCK)
    valid = blk < own
    qts = [qt_ref[hh] for hh in range(hb)]
    for hh in range(hb):
        k_mean = jnp.zeros((nbp, dh), F32)
        for j in range(nb):
            kj = k_ref[hh, j * MOBA_BLOCK:(j + 1) * MOBA_BLOCK, :].astype(F32)
            k_mean = jnp.where(sub == j, jnp.sum(kj, axis=0, keepdims=True) * (1.0 / MOBA_BLOCK), k_mean)
        km_hi = k_mean.astype(BF16)
        km_lo = (k_mean - km_hi.astype(F32)).astype(BF16)
        gate = (jnp.dot(km_hi, qts[hh], preferred_element_type=F32)
                + jnp.dot(km_lo, qts[hh], preferred_element_type=F32))[:nb]
        score = jnp.where(valid, gate, NEG_INF)
        sel = jnp.maximum(_topk_rows(score, valid, n_sel), jnp.where(blk == own, 1.0, 0.0))
        _fill_bias(bias_ref.at[hh], sel, 1)
    outs = _flash_t(qts, [k_ref.at[hh] for hh in range(hb)], [vt_ref.at[hh] for hh in range(hb)], q0,
                    tq=tq, tk=MOBA_BLOCK, scale=scale,
                    bias_refs=[bias_ref.at[hh] for hh in range(hb)], blk=MOBA_BLOCK)
    for hh in range(hb):
        o_ref[hh] = outs[hh].astype(o_ref.dtype)


def _moba_attention(qt, k, vt, tq=256, hb=4):
    b, hn, dh, s = qt.shape
    nb = s // MOBA_BLOCK
    n_sel = max(1, min(MOBA_TOPK, nb - 1))
    tq = min(tq, s)
    assert nb <= 2 * SUBLANES and hn % hb == 0
    qspec = pl.BlockSpec((None, hb, dh, tq), lambda bi, hi, i: (bi, hi, 0, i))
    kern = functools.partial(_moba_kernel, hb=hb, tq=tq, n_sel=n_sel, scale=dh ** -0.5)
    return pl.pallas_call(
        kern, out_shape=jax.ShapeDtypeStruct((b, hn, dh, s), BF16), grid=(b, hn // hb, s // tq),
        in_specs=[qspec,
                  pl.BlockSpec((None, hb, s, dh), lambda bi, hi, i: (bi, hi, 0, 0)),
                  pl.BlockSpec((None, hb, nb, dh, MOBA_BLOCK), lambda bi, hi, i: (bi, hi, 0, 0, 0))],
        out_specs=qspec,
        scratch_shapes=[pltpu.VMEM((hb, nb, SUBLANES, tq), F32)],
        compiler_params=_params("parallel", "parallel", "parallel"), name="moba_attention",
    )(qt, k, vt)


def _moba_mixer(h, b, s, tab_p, w_in):
    hn, dh = N_HEADS, HEAD_DIM
    assert s % MOBA_BLOCK == 0
    qk, v = _proj(h, [w_in[:, :2 * hn * dh].astype(BF16), w_in[:, 2 * hn * dh:].astype(BF16)], [BF16, BF16],
                  rope=[True, False], tables=tab_p, shift=ROPE_DIM // 2)
    qt = _heads_t(qk[:, :hn * dh], b, s, hn)
    k = _heads_first(qk[:, hn * dh:], b, s, hn)
    ot = _moba_attention(qt, k, _vt_blocks(v, b, s, hn, MOBA_BLOCK))
    return _tokens_major(ot, b, s)


def _swa_mixer(h, b, s, tab_p, w_in, sinks):
    hn, g, dh = N_HEADS, SWA_KV_HEADS, HEAD_DIM
    rep = hn // g
    tq = min(128, s)
    qk, v = _proj(h, [w_in[:, :(hn + g) * dh].astype(BF16), w_in[:, (hn + g) * dh:].astype(BF16)], [BF16, BF16],
                  rope=[True, False], tables=tab_p, shift=ROPE_DIM // 2)
    qt = _heads_t(qk[:, :hn * dh], b, s, hn).reshape(b, g, rep, dh, s)
    k = _heads_first(qk[:, hn * dh:], b, s, g)
    sink_rows = jnp.broadcast_to(sinks.astype(F32).reshape(g, rep, 1, 1), (g, rep, 1, tq))
    ot = _attention(qt, k, _vt_blocks(v, b, s, g, tq), scale=dh ** -0.5, window=SWA_WINDOW,
                    sinks=sink_rows, tq=tq)
    return _tokens_major(ot, b, s)


def _xattn_kernel(q_ref, kv_ref, o_ref, *, scale):
    d = q_ref.shape[-1]
    hd = XATTN_HEAD_DIM
    for hh in range(XATTN_HEADS):
        q = q_ref[:, hh * hd:(hh + 1) * hd]
        k = kv_ref[:, hh * hd:(hh + 1) * hd]
        v = kv_ref[:, d + hh * hd:d + (hh + 1) * hd]
        s = lax.dot_general(q, k, (((1,), (1,)), ((), ())), preferred_element_type=F32) * scale
        m = jnp.max(s, axis=-1, keepdims=True)
        e = jnp.exp(s - m)
        p = e / jnp.sum(e, axis=-1, keepdims=True)
        o_ref[:, hh * hd:(hh + 1) * hd] = jnp.dot(p.astype(BF16), v, preferred_element_type=F32).astype(o_ref.dtype)


def _cross_attention(q, kv, b, s, tq=512):
    d = q.shape[-1]
    mlen = kv.shape[0] // b
    tq = min(tq, s)
    q3 = q.reshape(b, s, d)
    kv3 = kv.reshape(b, mlen, 2 * d)
    o = pl.pallas_call(
        functools.partial(_xattn_kernel, scale=XATTN_HEAD_DIM ** -0.5),
        out_shape=jax.ShapeDtypeStruct((b, s, d), BF16), grid=(b, s // tq),
        in_specs=[pl.BlockSpec((None, tq, d), lambda bi, i: (bi, i, 0)),
                  pl.BlockSpec((None, mlen, 2 * d), lambda bi, i: (bi, 0, 0))],
        out_specs=pl.BlockSpec((None, tq, d), lambda bi, i: (bi, i, 0)),
        compiler_params=_params("parallel", "parallel"), name="cross_attention",
    )(q3, kv3)
    return o.reshape(b * s, d)


def _swiglu_partial(xb, wg_ref, wu_ref, wo_ref):
    gte = jnp.dot(xb, wg_ref[...], preferred_element_type=F32)
    up = jnp.dot(xb, wu_ref[...], preferred_element_type=F32)
    act = gte * jax.nn.sigmoid(gte) * up
    return jnp.dot(act.astype(BF16), wo_ref[...], preferred_element_type=F32)


def _ffn_ln_kernel(x_ref, wg_ref, wu_ref, wo_ref, g_ref, b_ref, o_ref, acc_ref):
    f = pl.program_id(1)

    @pl.when(f == 0)
    def _():
        acc_ref[...] = jnp.zeros_like(acc_ref)

    acc_ref[...] += _swiglu_partial(x_ref[...].astype(BF16), wg_ref, wu_ref, wo_ref)

    @pl.when(f == pl.num_programs(1) - 1)
    def _():
        o_ref[...] = _ln(DEEPNORM_ALPHA * x_ref[...] + acc_ref[...], g_ref[...], b_ref[...])


def _ffn_ln(h, w_in, w_out, g, b, tm=1024, tf=512):
    m, d = h.shape
    dff = w_out.shape[0]
    tm = min(tm, m)
    nf = dff // tf
    row = pl.BlockSpec((tm, d), lambda i, f: (i, 0))
    vec = pl.BlockSpec((1, d), lambda i, f: (0, 0))
    return pl.pallas_call(
        _ffn_ln_kernel, out_shape=jax.ShapeDtypeStruct((m, d), F32), grid=(m // tm, nf),
        in_specs=[row,
                  pl.BlockSpec((d, tf), lambda i, f: (0, f)),
                  pl.BlockSpec((d, tf), lambda i, f: (0, nf + f)),
                  pl.BlockSpec((tf, d), lambda i, f: (f, 0)),
                  vec, vec],
        out_specs=row, scratch_shapes=[pltpu.VMEM((tm, d), F32)],
        compiler_params=_params("parallel", "arbitrary"), name="ffn_ln",
    )(h, w_in.astype(BF16), w_in.astype(BF16), w_out.astype(BF16), g.reshape(1, d), b.reshape(1, d))


def _router_kernel(x_ref, rh_ref, rl_ref, b_ref, comb_ref, sel_ref):
    x = x_ref[...]
    xh = x.astype(BF16)
    xl = (x - xh.astype(F32)).astype(BF16)
    rh, rl = rh_ref[...], rl_ref[...]
    lg = (jnp.dot(xh, rh, preferred_element_type=F32) + jnp.dot(xl, rh, preferred_element_type=F32)
          + jnp.dot(xh, rl, preferred_element_type=F32)) + b_ref[...]
    lane = lax.broadcasted_iota(jnp.int32, lg.shape, 1).astype(F32)
    lg = jnp.where(lane < N_EXPERTS, lg, NEG_INF)
    m1 = jnp.max(lg, axis=-1, keepdims=True)
    i1 = jnp.min(jnp.where(lg == m1, lane, float(LANES)), axis=-1, keepdims=True)
    lg2 = jnp.where(lane == i1, NEG_INF, lg)
    m2 = jnp.max(lg2, axis=-1, keepdims=True)
    i2 = jnp.min(jnp.where(lg2 == m2, lane, float(LANES)), axis=-1, keepdims=True)
    e2 = jnp.exp(m2 - m1)
    den = 1.0 + e2
    comb_ref[...] = jnp.where(lane == i1, 1.0 / den, 0.0) + jnp.where(lane == i2, e2 / den, 0.0)
    sel_ref[...] = jnp.where((lane == i1) | (lane == i2), 1.0, 0.0)


def _router(h, router, bias, tm=512):
    m, d = h.shape
    tm = min(tm, m)
    rp = jnp.pad(router.astype(F32), ((0, 0), (0, LANES - N_EXPERTS)))
    rh = rp.astype(BF16)
    rl = (rp - rh.astype(F32)).astype(BF16)
    bp = jnp.pad(bias.astype(F32), (0, LANES - N_EXPERTS)).reshape(1, LANES)
    row = pl.BlockSpec((tm, d), lambda i: (i, 0))
    wsp = pl.BlockSpec((d, LANES), lambda i: (0, 0))
    osp = pl.BlockSpec((tm, LANES), lambda i: (i, 0))
    return pl.pallas_call(
        _router_kernel, out_shape=[jax.ShapeDtypeStruct((m, LANES), F32)] * 2, grid=(m // tm,),
        in_specs=[row, wsp, wsp, pl.BlockSpec((1, LANES), lambda i: (0, 0))], out_specs=[osp, osp],
        compiler_params=_params("parallel"), name="moe_router",
    )(h, rh, rl, bp)


SCALAR_UNROLL = 8


def _slot_tokens_kernel(pos_ref, src_ref):
    t = pos_ref.shape[0] // 2

    def zero(i, c):
        src_ref[i] = 0
        return c

    lax.fori_loop(0, src_ref.shape[0], zero, 0, unroll=SCALAR_UNROLL)

    def body(i, c):
        src_ref[pos_ref[i]] = i
        src_ref[pos_ref[t + i]] = i
        return c

    lax.fori_loop(0, t, body, 0, unroll=SCALAR_UNROLL)


def _slot_tokens(pos2, tp):
    smem = pl.BlockSpec(memory_space=pltpu.SMEM)
    return pl.pallas_call(
        _slot_tokens_kernel, out_shape=jax.ShapeDtypeStruct((tp,), jnp.int32),
        in_specs=[smem], out_specs=smem, name="moe_slot_tokens",
    )(pos2)


def _moe_ffn_kernel(te_ref, tv_ref, x_ref, wg_ref, wu_ref, wo_ref, o_ref, acc_ref):
    i = pl.program_id(0)
    f = pl.program_id(1)

    @pl.when(f == 0)
    def _():
        acc_ref[...] = jnp.zeros_like(acc_ref)

    @pl.when(tv_ref[i] > 0)
    def _():
        acc_ref[...] += _swiglu_partial(x_ref[...].astype(BF16), wg_ref, wu_ref, wo_ref)

    @pl.when(f == pl.num_programs(1) - 1)
    def _():
        o_ref[...] = acc_ref[...].astype(o_ref.dtype)


def _moe_ffn(xs, w_in, w_out, tile_expert, tile_valid, tm, tf=512):
    tp, d = xs.shape
    dff = w_out.shape[1]
    nf = dff // tf
    last = nf - 1

    def fblk(f, tv, i):
        return jnp.where(tv[i] > 0, f, last)

    grid_spec = pltpu.PrefetchScalarGridSpec(
        num_scalar_prefetch=2, grid=(tp // tm, nf),
        in_specs=[pl.BlockSpec((tm, d), lambda i, f, te, tv: (i, 0)),
                  pl.BlockSpec((None, d, tf), lambda i, f, te, tv: (te[i], 0, fblk(f, tv, i))),
                  pl.BlockSpec((None, d, tf), lambda i, f, te, tv: (te[i], 0, nf + fblk(f, tv, i))),
                  pl.BlockSpec((None, tf, d), lambda i, f, te, tv: (te[i], fblk(f, tv, i), 0))],
        out_specs=pl.BlockSpec((tm, d), lambda i, f, te, tv: (i, 0)),
        scratch_shapes=[pltpu.VMEM((tm, d), F32)])
    return pl.pallas_call(
        _moe_ffn_kernel, out_shape=jax.ShapeDtypeStruct((tp, d), BF16), grid_spec=grid_spec,
        compiler_params=_params("parallel", "arbitrary"), name="moe_ffn",
    )(tile_expert, tile_valid, xs, w_in, w_in, w_out)


def _moe_layer(h, router, bias, w_in, w_out, g, b, tm=512):
    t, d = h.shape
    tm = min(tm, t)
    comb, sel = _router(h, router, bias)
    comb, sel = comb[:, :N_EXPERTS], sel[:, :N_EXPERTS] > 0.5

    seli = sel.astype(jnp.int32)
    cnt = jnp.sum(seli, axis=0)
    ntile = (cnt + tm - 1) // tm
    tile_end = jnp.cumsum(ntile)
    row_start = (tile_end - ntile) * tm
    rank = jnp.cumsum(seli, axis=0) - seli
    n_tiles = (t * TOP_K) // tm + N_EXPERTS
    tp = n_tiles * tm
    pos = jnp.where(sel, row_start[None, :] + rank, tp)
    tiles = jnp.arange(n_tiles, dtype=jnp.int32)
    tile_valid = (tiles < tile_end[-1]).astype(jnp.int32)
    tile_expert = jnp.sum((tiles[:, None] >= tile_end[None, :]).astype(jnp.int32), axis=1)
    last_e = jnp.sum((tile_end[-1] - 1 >= tile_end).astype(jnp.int32))
    tile_expert = jnp.where(tile_valid > 0, tile_expert, last_e).astype(jnp.int32)
    p_lo = jnp.min(pos, axis=1)
    p_hi = jnp.max(jnp.where(sel, pos, -1), axis=1)
    w2 = jnp.stack([jnp.sum(jnp.where(pos == p_lo[:, None], comb, 0.0), axis=1),
                    jnp.sum(jnp.where(pos == p_hi[:, None], comb, 0.0), axis=1)], axis=1)
    pos2 = jnp.concatenate([p_lo, p_hi])
    xs = jnp.take(h, _slot_tokens(pos2, tp), axis=0)
    ys = _moe_ffn(xs, w_in.astype(BF16), w_out.astype(BF16), tile_expert, tile_valid, tm)
    return _combine_ln(h, jnp.take(ys, pos2, axis=0), w2, g, b)


def kernel(x, mem, positions, l0_nsa_w_in, l0_nsa_cmp_pe, l0_nsa_cmp_w1, l0_nsa_cmp_w2, l0_nsa_w_out, l0_ln1_g, l0_ln1_b, l0_xq, l0_xkv, l0_xo, l0_ln2_g, l0_ln2_b, l0_ffn_w_in, l0_ffn_w_out, l0_ln3_g, l0_ln3_b, l1_mla_w_down, l1_mla_q_norm, l1_mla_kv_norm, l1_mla_w_uq, l1_mla_w_ukv, l1_mla_w_out, l1_ln1_g, l1_ln1_b, l1_xq, l1_xkv, l1_xo, l1_ln2_g, l1_ln2_b, l1_moe_router, l1_moe_bias, l1_moe_w_in, l1_moe_w_out, l1_ln3_g, l1_ln3_b, l2_moba_w_in, l2_moba_w_out, l2_ln1_g, l2_ln1_b, l2_xq, l2_xkv, l2_xo, l2_ln2_g, l2_ln2_b, l2_ffn_w_in, l2_ffn_w_out, l2_ln3_g, l2_ln3_b, l3_swa_w_in, l3_swa_sinks, l3_swa_w_out, l3_ln1_g, l3_ln1_b, l3_xq, l3_xkv, l3_xo, l3_ln2_g, l3_ln2_b, l3_moe_router, l3_moe_bias, l3_moe_w_in, l3_moe_w_out, l3_ln3_g, l3_ln3_b):
    b, s, d = x.shape
    t = b * s
    tab_p = _rope_tables(positions, ROPE_DIM, HEAD_DIM)
    tab_m = _rope_tables(positions, MLA_ROPE_DIM, MLA_ROPE_DIM)
    mem2 = mem.reshape(-1, d)

    mixers = [
        (lambda h: _nsa_mixer(h, b, s, tab_p, l0_nsa_w_in, l0_nsa_cmp_pe, l0_nsa_cmp_w1, l0_nsa_cmp_w2), l0_nsa_w_out),
        (lambda h: _mla_mixer(h, b, s, tab_m, l1_mla_w_down, l1_mla_q_norm, l1_mla_kv_norm, l1_mla_w_uq, l1_mla_w_ukv),
         l1_mla_w_out),
        (lambda h: _moba_mixer(h, b, s, tab_p, l2_moba_w_in), l2_moba_w_out),
        (lambda h: _swa_mixer(h, b, s, tab_p, l3_swa_w_in, l3_swa_sinks), l3_swa_w_out),
    ]
    ln1 = [(l0_ln1_g, l0_ln1_b), (l1_ln1_g, l1_ln1_b), (l2_ln1_g, l2_ln1_b), (l3_ln1_g, l3_ln1_b)]
    xattn = [(l0_xq, l0_xkv, l0_xo), (l1_xq, l1_xkv, l1_xo), (l2_xq, l2_xkv, l2_xo), (l3_xq, l3_xkv, l3_xo)]
    ln2 = [(l0_ln2_g, l0_ln2_b), (l1_ln2_g, l1_ln2_b), (l2_ln2_g, l2_ln2_b), (l3_ln2_g, l3_ln2_b)]
    ln3 = [(l0_ln3_g, l0_ln3_b), (l1_ln3_g, l1_ln3_b), (l2_ln3_g, l2_ln3_b), (l3_ln3_g, l3_ln3_b)]
    ffns = [
        lambda h, g, bb: _ffn_ln(h, l0_ffn_w_in, l0_ffn_w_out, g, bb),
        lambda h, g, bb: _moe_layer(h, l1_moe_router, l1_moe_bias, l1_moe_w_in, l1_moe_w_out, g, bb),
        lambda h, g, bb: _ffn_ln(h, l2_ffn_w_in, l2_ffn_w_out, g, bb),
        lambda h, g, bb: _moe_layer(h, l3_moe_router, l3_moe_bias, l3_moe_w_in, l3_moe_w_out, g, bb),
    ]

    h = x.reshape(t, d)
    for i in range(DEPTH):
        mixer, w_out = mixers[i]
        h = _mm_res_ln(mixer(h), w_out.astype(BF16), h, *ln1[i])
        w_q, w_kv, w_o = xattn[i]
        (q,) = _proj(h, [w_q.astype(BF16)], [BF16])
        (kv,) = _proj(mem2, [w_kv.astype(BF16)], [BF16])
        h = _mm_res_ln(_cross_attention(q, kv, b, s), w_o.astype(BF16), h, *ln2[i])
        h = ffns[i](h, *ln3[i])
    return h.reshape(b, s, d)
```
